```python
import math
import jax
import jax.numpy as jnp
from jax import lax
import numpy as np

D_MODEL = 1024
BATCH = 16
SEQ = 4096
DEPTH = 1
DEC_BATCH = 128
DEC_SEQ = 4
PAST_LEN = 8192
PAGE_SIZE = 128

SSM_WIDTH = D_MODEL // 2
SSM_GROUP = 16
SSM_GROUPS = SSM_WIDTH // SSM_GROUP
SSM_STATE = 64
ATT_WIDTH = D_MODEL - SSM_WIDTH
ATT_HEADS = 4
HEAD_DIM = ATT_WIDTH // (2 * ATT_HEADS)
V_DIM = 2 * HEAD_DIM
IN_WIDTH = SSM_WIDTH + 3 * ATT_WIDTH
N_EXPERTS = 32
TOP_K = 4
D_FF = D_MODEL
SWIGLU_LIMIT = 7.0
SWIGLU_ALPHA = 1.702
MOE_BLOCK = 128
Q_BLOCK = 128
RMS_EPS = 1e-6
DT_MIN = 1e-3
DT_MAX = 1e-1

kernel_name = 'hybrid_s5_diffattn_moe_step'


def rms_norm(x, w):
    x32 = x.astype(jnp.float32)
    y = x32 * lax.rsqrt(jnp.mean(x32 * x32, axis=-1, keepdims=True) + RMS_EPS)
    return (y * w.astype(jnp.float32)).astype(x.dtype)


def _ssm_combine(left, right):
    a_l, b_l = left
    a_r, b_r = right
    return a_r * a_l, a_r * b_l + b_r


def s5_group(u, h0, lam_re, lam_im, log_dt, b_re, b_im, c_re, c_im, d_skip, glu_w, glu_b, norm_w):
    n, l, _ = u.shape
    f32 = jnp.float32
    lam = lax.complex(lam_re.astype(f32), lam_im.astype(f32))
    dt = jnp.exp(log_dt.astype(f32))[:, None]
    a_bar = jnp.exp(lam * dt)
    b_bar = ((a_bar - 1.0) / lam)[:, :, None] * lax.complex(b_re.astype(f32), b_im.astype(f32))
    c_mat = lax.complex(c_re.astype(f32), c_im.astype(f32))
    u32 = u.astype(f32).reshape(n, l, SSM_GROUPS, SSM_GROUP)
    bu = jnp.einsum('nlgc,gpc->nlgp', u32.astype(jnp.complex64), b_bar)
    bu = bu.at[:, 0].add(a_bar * h0)
    a = jnp.broadcast_to(a_bar, bu.shape)
    _, h = lax.associative_scan(_ssm_combine, (a, bu), axis=1)
    y = jnp.einsum('nlgp,gcp->nlgc', h, c_mat).real + d_skip.astype(f32).reshape(SSM_GROUPS, SSM_GROUP) * u32
    y = jax.nn.gelu(y.reshape(n, l, SSM_WIDTH))
    y = y * jax.nn.sigmoid(y @ glu_w.astype(f32) + glu_b.astype(f32))
    return rms_norm(y, norm_w).astype(u.dtype), h[:, -1]


def diff_lambda(lq1, lk1, lq2, lk2, lam_init):
    f32 = jnp.float32
    return (jnp.exp(jnp.sum(lq1.astype(f32) * lk1.astype(f32)))
            - jnp.exp(jnp.sum(lq2.astype(f32) * lk2.astype(f32))) + lam_init)


def _diff_probs(s, lam, dtype):
    p = jax.nn.softmax(s, axis=-1)
    return (p[:, :, 0] - lam * p[:, :, 1]).astype(dtype)


def diff_attn_prompt(q, k, v, lam):
    n, l = q.shape[:2]
    nb = l // Q_BLOCK
    scale = HEAD_DIM ** -0.5
    qb = q.reshape(n, nb, Q_BLOCK, ATT_HEADS, 2, HEAD_DIM).transpose(1, 0, 2, 3, 4, 5)
    k_pos = jnp.arange(l)

    def block(args):
        i, qi = args
        s = jnp.einsum('bqhcd,bkhcd->bhcqk', qi, k).astype(jnp.float32) * scale
        q_pos = i * Q_BLOCK + jnp.arange(Q_BLOCK)
        s = jnp.where(k_pos[None, :] <= q_pos[:, None], s, -jnp.inf)
        return jnp.einsum('bhqk,bkhe->bqhe', _diff_probs(s, lam, v.dtype), v)

    o = lax.map(block, (jnp.arange(nb), qb))
    return o.transpose(1, 0, 2, 3, 4).reshape(n, l, ATT_HEADS, V_DIM)


def diff_attn_sample(q, k_new, v_new, k_past, v_past, lam):
    t = q.shape[1]
    past = k_past.shape[1]
    scale = HEAD_DIM ** -0.5
    s_past = jnp.einsum('bqhcd,bkhcd->bhcqk', q, k_past).astype(jnp.float32) * scale
    s_new = jnp.einsum('bqhcd,bkhcd->bhcqk', q, k_new).astype(jnp.float32) * scale
    causal = jnp.arange(t)[None, :] <= jnp.arange(t)[:, None]
    s_new = jnp.where(causal, s_new, -jnp.inf)
    pd = _diff_probs(jnp.concatenate([s_past, s_new], axis=-1), lam, v_new.dtype)
    return (jnp.einsum('bhqk,bkhe->bqhe', pd[..., :past], v_past)
            + jnp.einsum('bhqk,bkhe->bqhe', pd[..., past:], v_new))


def moe(x, router_w, router_b, w_up, b_up, w_down, b_down):
    shape = x.shape
    xt = x.reshape(-1, D_MODEL)
    t = xt.shape[0]
    logits = (xt @ router_w + router_b).astype(jnp.float32)
    top_val, top_idx = lax.top_k(logits, TOP_K)
    gates = jax.nn.softmax(top_val, axis=-1)
    flat_e = top_idx.reshape(-1)
    flat_tok = jnp.repeat(jnp.arange(t, dtype=jnp.int32), TOP_K)
    flat_gate = gates.reshape(-1)
    order = jnp.argsort(flat_e)
    sorted_e = flat_e[order]
    counts = jnp.bincount(flat_e, length=N_EXPERTS)
    padded = ((counts + MOE_BLOCK - 1) // MOE_BLOCK) * MOE_BLOCK
    starts = jnp.cumsum(counts) - counts
    ends_p = jnp.cumsum(padded)
    starts_p = ends_p - padded
    dest = starts_p[sorted_e] + (jnp.arange(flat_e.shape[0]) - starts[sorted_e])
    n_blocks = -(-(t * TOP_K + N_EXPERTS * (MOE_BLOCK - 1)) // MOE_BLOCK)
    n_rows = n_blocks * MOE_BLOCK
    row_tok = jnp.full((n_rows,), t, jnp.int32).at[dest].set(flat_tok[order])
    row_gate = jnp.zeros((n_rows,), jnp.float32).at[dest].set(flat_gate[order])
    block_exp = jnp.clip(jnp.searchsorted(ends_p, jnp.arange(n_blocks) * MOE_BLOCK, side='right'), 0, N_EXPERTS - 1)
    x_pad = jnp.concatenate([xt, jnp.zeros((1, D_MODEL), xt.dtype)], axis=0)

    def expert_block(args):
        e, tok, g = args
        xb = x_pad[tok]
        h = (xb @ w_up[e] + b_up[e]).astype(jnp.float32)
        h_glu = jnp.minimum(h[:, 0::2], SWIGLU_LIMIT)
        h_lin = jnp.clip(h[:, 1::2], -SWIGLU_LIMIT, SWIGLU_LIMIT)
        act = h_glu * jax.nn.sigmoid(SWIGLU_ALPHA * h_glu) * (h_lin + 1.0)
        out = (act.astype(xb.dtype) @ w_down[e] + b_down[e]).astype(jnp.float32)
        return out * g[:, None]

    rows = lax.map(expert_block, (block_exp, row_tok.reshape(n_blocks, MOE_BLOCK), row_gate.reshape(n_blocks, MOE_BLOCK)))
    y = jnp.zeros((t + 1, D_MODEL), jnp.float32).at[row_tok].add(rows.reshape(n_rows, D_MODEL))
    return y[:t].astype(x.dtype).reshape(shape)


def _layer(x, h0, k_past, v_past, p, lam_init):
    n, l, _ = x.shape
    xn = rms_norm(x, p['attn_norm_w'])
    proj = xn @ p['w_in']
    u, q, k, v = jnp.split(proj, [SSM_WIDTH, SSM_WIDTH + ATT_WIDTH, SSM_WIDTH + 2 * ATT_WIDTH], axis=-1)
    y_ssm, h_last = s5_group(u, h0, p['lam_re'], p['lam_im'], p['log_dt'], p['b_re'], p['b_im'],
                             p['c_re'], p['c_im'], p['d'], p['glu_w'], p['glu_b'], p['ssm_norm_w'])
    q = q.reshape(n, l, ATT_HEADS, 2, HEAD_DIM)
    k = k.reshape(n, l, ATT_HEADS, 2, HEAD_DIM)
    v = v.reshape(n, l, ATT_HEADS, V_DIM)
    lam = diff_lambda(p['lq1'], p['lk1'], p['lq2'], p['lk2'], lam_init)
    if k_past is None:
        o = diff_attn_prompt(q, k, v, lam)
    else:
        o = diff_attn_sample(q, k, v, k_past, v_past, lam)
    o = rms_norm(o, p['subln_w']) * (1.0 - lam_init)
    mixed = jnp.concatenate([y_ssm, o.reshape(n, l, ATT_WIDTH).astype(x.dtype)], axis=-1) @ p['w_out']
    h = x + mixed
    h = h + moe(rms_norm(h, p['ffn_norm_w']), p['router_w'], p['router_b'], p['w_up'], p['b_up'], p['w_down'], p['b_down'])
    return h, k, v, h_last


def setup_inputs(seed: int = 0) -> dict:
    key = jax.random.key(seed)
    ks = jax.random.split(key, 40)
    f32 = jnp.float32
    n_pages = PAST_LEN // PAGE_SIZE
    n_used = DEC_BATCH * n_pages
    n_pool = n_used + n_used // 4

    def nrm(k, shape, scale):
        return jax.random.normal(k, shape, f32) * scale

    def gain(k, shape):
        return 1.0 + 0.05 * jax.random.normal(k, shape, f32)

    n_idx = jnp.arange(SSM_STATE, dtype=f32)
    page_table = jax.random.permutation(ks[6], n_pool)[:n_used].reshape(DEC_BATCH, n_pages).astype(jnp.int32)
    return {
        'x_prompt': nrm(ks[0], (BATCH, SEQ, D_MODEL), 1.0),
        'x_sample': nrm(ks[1], (DEC_BATCH, DEC_SEQ, D_MODEL), 1.0),
        'cache_k': nrm(ks[2], (DEPTH, n_pool, PAGE_SIZE, ATT_HEADS, 2, HEAD_DIM), 1.0),
        'cache_v': nrm(ks[3], (DEPTH, n_pool, PAGE_SIZE, ATT_HEADS, V_DIM), 1.0),
        'state_ssm_re': nrm(ks[4], (DEPTH, DEC_BATCH, SSM_GROUPS, SSM_STATE), 0.5),
        'state_ssm_im': nrm(ks[5], (DEPTH, DEC_BATCH, SSM_GROUPS, SSM_STATE), 0.5),
        'page_table': page_table,
        'attn_norm_w': gain(ks[7], (DEPTH, D_MODEL)),
        'w_in': nrm(ks[8], (DEPTH, D_MODEL, IN_WIDTH), D_MODEL ** -0.5),
        'ssm_lambda_re': -0.5 + nrm(ks[9], (DEPTH, SSM_GROUPS, SSM_STATE), 0.01),
        'ssm_lambda_im': math.pi * n_idx + nrm(ks[10], (DEPTH, SSM_GROUPS, SSM_STATE), 0.01),
        'ssm_log_dt': jax.random.uniform(ks[11], (DEPTH, SSM_GROUPS), f32, math.log(DT_MIN), math.log(DT_MAX)),
        'ssm_b_re': nrm(ks[12], (DEPTH, SSM_GROUPS, SSM_STATE, SSM_GROUP), (2 * SSM_GROUP) ** -0.5),
        'ssm_b_im': nrm(ks[13], (DEPTH, SSM_GROUPS, SSM_STATE, SSM_GROUP), (2 * SSM_GROUP) ** -0.5),
        'ssm_c_re': nrm(ks[14], (DEPTH, SSM_GROUPS, SSM_GROUP, SSM_STATE), (SSM_STATE / 4) ** -0.5),
        'ssm_c_im': nrm(ks[15], (DEPTH, SSM_GROUPS, SSM_GROUP, SSM_STATE), (SSM_STATE / 4) ** -0.5),
        'ssm_d': nrm(ks[16], (DEPTH, SSM_WIDTH), 1.0),
        'ssm_glu_w': nrm(ks[17], (DEPTH, SSM_WIDTH, SSM_WIDTH), SSM_WIDTH ** -0.5),
        'ssm_glu_b': nrm(ks[18], (DEPTH, SSM_WIDTH), 0.01),
        'ssm_norm_w': gain(ks[19], (DEPTH, SSM_WIDTH)),
        'diff_lambda_q1': nrm(ks[20], (DEPTH, HEAD_DIM), 0.1),
        'diff_lambda_k1': nrm(ks[21], (DEPTH, HEAD_DIM), 0.1),
        'diff_lambda_q2': nrm(ks[22], (DEPTH, HEAD_DIM), 0.1),
        'diff_lambda_k2': nrm(ks[23], (DEPTH, HEAD_DIM), 0.1),
        'subln_w': gain(ks[24], (DEPTH, V_DIM)),
        'w_out': nrm(ks[25], (DEPTH, D_MODEL, D_MODEL), D_MODEL ** -0.5),
        'ffn_norm_w': gain(ks[26], (DEPTH, D_MODEL)),
        'router_w': nrm(ks[27], (DEPTH, D_MODEL, N_EXPERTS), D_MODEL ** -0.5),
        'router_b': nrm(ks[28], (DEPTH, N_EXPERTS), 0.01),
        'w_up': nrm(ks[29], (DEPTH, N_EXPERTS, D_MODEL, 2 * D_FF), D_MODEL ** -0.5),
        'b_up': nrm(ks[30], (DEPTH, N_EXPERTS, 2 * D_FF), 0.01),
        'w_down': nrm(ks[31], (DEPTH, N_EXPERTS, D_FF, D_MODEL), D_FF ** -0.5),
        'b_down': nrm(ks[32], (DEPTH, N_EXPERTS, D_MODEL), 0.01),
        'final_norm_w': gain(ks[33], (D_MODEL,)),
    }


def reference(x_prompt, x_sample, cache_k, cache_v, state_ssm_re, state_ssm_im, page_table,
              attn_norm_w, w_in, ssm_lambda_re, ssm_lambda_im, ssm_log_dt, ssm_b_re, ssm_b_im,
              ssm_c_re, ssm_c_im, ssm_d, ssm_glu_w, ssm_glu_b, ssm_norm_w,
              diff_lambda_q1, diff_lambda_k1, diff_lambda_q2, diff_lambda_k2, subln_w, w_out,
              ffn_norm_w, router_w, router_b, w_up, b_up, w_down, b_down, final_norm_w):
    f32 = jnp.float32
    hp = x_prompt
    hs = x_sample
    n_s = x_sample.shape[0]
    kp_l, vp_l, sre_p, sim_p = [], [], [], []
    ks_l, vs_l, sre_s, sim_s = [], [], [], []
    for l in range(DEPTH):
        p = {
            'attn_norm_w': attn_norm_w[l], 'w_in': w_in[l],
            'lam_re': ssm_lambda_re[l], 'lam_im': ssm_lambda_im[l], 'log_dt': ssm_log_dt[l],
            'b_re': ssm_b_re[l], 'b_im': ssm_b_im[l], 'c_re': ssm_c_re[l], 'c_im': ssm_c_im[l],
            'd': ssm_d[l], 'glu_w': ssm_glu_w[l], 'glu_b': ssm_glu_b[l], 'ssm_norm_w': ssm_norm_w[l],
            'lq1': diff_lambda_q1[l], 'lk1': diff_lambda_k1[l], 'lq2': diff_lambda_q2[l], 'lk2': diff_lambda_k2[l],
            'subln_w': subln_w[l], 'w_out': w_out[l], 'ffn_norm_w': ffn_norm_w[l],
            'router_w': router_w[l], 'router_b': router_b[l], 'w_up': w_up[l], 'b_up': b_up[l],
            'w_down': w_down[l], 'b_down': b_down[l],
        }
        lam_init = 0.8 - 0.6 * math.exp(-0.3 * l)
        h0p = jnp.zeros((hp.shape[0], SSM_GROUPS, SSM_STATE), jnp.complex64)
        hp, kp, vp, stp = _layer(hp, h0p, None, None, p, lam_init)
        h0s = lax.complex(state_ssm_re[l].astype(f32), state_ssm_im[l].astype(f32))
        k_past = cache_k[l, page_table].reshape(n_s, -1, ATT_HEADS, 2, HEAD_DIM)
        v_past = cache_v[l, page_table].reshape(n_s, -1, ATT_HEADS, V_DIM)
        hs, kn, vn, sts = _layer(hs, h0s, k_past, v_past, p, lam_init)
        kp_l.append(kp)
        vp_l.append(vp)
        sre_p.append(stp.real)
        sim_p.append(stp.imag)
        ks_l.append(kn)
        vs_l.append(vn)
        sre_s.append(sts.real)
        sim_s.append(sts.imag)
    y_prompt = rms_norm(hp, final_norm_w)
    y_sample = rms_norm(hs, final_norm_w)
    return (y_prompt, y_sample, jnp.stack(kp_l), jnp.stack(vp_l), jnp.stack(sre_p), jnp.stack(sim_p),
            jnp.stack(ks_l), jnp.stack(vs_l), jnp.stack(sre_s), jnp.stack(sim_s))
```

```python
import functools
import math

import jax
import jax.numpy as jnp
from jax import lax
from jax.experimental import pallas as pl
from jax.experimental.pallas import tpu as pltpu

F32 = jnp.float32
BF16 = jnp.bfloat16

SSM_GROUP = 16
SSM_STATE = 64
ATT_HEADS = 4
TOP_K = 4
SWIGLU_LIMIT = 7.0
SWIGLU_ALPHA = 1.702
RMS_EPS = 1e-6
NEG_BIG = -1e30

LANES = 128
SUBLANES = 8
VMEM_LIMIT = 56 * 1024 * 1024

INPROJ_ROWS = 512
ATTN_BLOCK = 256
S5_CHUNK = 16
S5_ROWS = 512
TOKEN_TILE = 256
FFN_ROWS = 512
UNIT = SUBLANES
DEC_PAGES = 8


def _cparams(sem):
    return pltpu.CompilerParams(dimension_semantics=sem, vmem_limit_bytes=VMEM_LIMIT)


def _rms(x, w):
    ms = jnp.mean(x * x, axis=-1, keepdims=True)
    return x * lax.rsqrt(ms + RMS_EPS) * w


def _dot(a, b):
    return jnp.dot(a, b, preferred_element_type=F32)


def _dot_nt(a, b):
    return lax.dot_general(a, b, (((1,), (1,)), ((), ())), preferred_element_type=F32)


def _inproj_body(x_ref, nw_ref, w_ref, u_ref, k_ref, v_ref, qb_ref, kb_ref, vb_ref, *, sw, aw, scale):
    xn = _rms(x_ref[...], nw_ref[...])
    proj = _dot(xn.astype(BF16), w_ref[...])
    u_ref[...] = proj[:, :sw]
    q = proj[:, sw:sw + aw]
    k = proj[:, sw + aw:sw + 2 * aw]
    v = proj[:, sw + 2 * aw:]
    k_ref[...] = k
    v_ref[...] = v
    qb_ref[...] = (q * scale).astype(BF16)
    kb_ref[...] = k.astype(BF16)
    vb_ref[...] = v.astype(BF16)


def _inproj(x, norm_w, w_in_b, sw, aw, scale):
    t, d = x.shape
    tm = min(INPROJ_ROWS, t)
    row = lambda i: (i, 0)
    fix = lambda i: (0, 0)
    return pl.pallas_call(
        functools.partial(_inproj_body, sw=sw, aw=aw, scale=scale),
        grid=(t // tm,),
        in_specs=[pl.BlockSpec((tm, d), row), pl.BlockSpec((1, d), fix), pl.BlockSpec(w_in_b.shape, fix)],
        out_specs=[pl.BlockSpec((tm, sw), row)] + [pl.BlockSpec((tm, aw), row)] * 5,
        out_shape=[jax.ShapeDtypeStruct((t, sw), F32), jax.ShapeDtypeStruct((t, aw), F32),
                   jax.ShapeDtypeStruct((t, aw), F32), jax.ShapeDtypeStruct((t, aw), BF16),
                   jax.ShapeDtypeStruct((t, aw), BF16), jax.ShapeDtypeStruct((t, aw), BF16)],
        compiler_params=_cparams(("parallel",)),
    )(x, norm_w.reshape(1, d), w_in_b)


def _s5_params(lam_re, lam_im, log_dt, b_re, b_im, c_re, c_im, d_skip, tc):
    hp = lax.Precision.HIGHEST
    g, p = lam_re.shape
    c = b_re.shape[-1]
    dt = jnp.exp(log_dt)[:, None]
    mag = jnp.exp(lam_re * dt)
    a_re = mag * jnp.cos(lam_im * dt)
    a_im = mag * jnp.sin(lam_im * dt)
    den = lam_re * lam_re + lam_im * lam_im
    nr = a_re - 1.0
    cf_re = (nr * lam_re + a_im * lam_im) / den
    cf_im = (a_im * lam_re - nr * lam_im) / den
    bb_re = cf_re[..., None] * b_re - cf_im[..., None] * b_im
    bb_im = cf_re[..., None] * b_im + cf_im[..., None] * b_re
    pr, pi = [jnp.ones_like(a_re)], [jnp.zeros_like(a_re)]
    for _ in range(tc):
        pr, pi = pr + [pr[-1] * a_re - pi[-1] * a_im], pi + [pr[-1] * a_im + pi[-1] * a_re]
    pr = jnp.stack(pr)
    pi = jnp.stack(pi)
    ab_re = pr[:tc, :, :, None] * bb_re - pi[:tc, :, :, None] * bb_im
    ab_im = pr[:tc, :, :, None] * bb_im + pi[:tc, :, :, None] * bb_re
    kern = (jnp.einsum('gdp,tgpc->tgdc', c_re, ab_re, precision=hp)
            - jnp.einsum('gdp,tgpc->tgdc', c_im, ab_im, precision=hp))
    lag = jnp.arange(tc)[None, :] - jnp.arange(tc)[:, None]
    kt = kern[jnp.clip(lag, 0, tc - 1)]
    kt = jnp.where((lag >= 0)[:, :, None, None, None], kt, 0.0)
    m = kt.transpose(2, 0, 4, 1, 3).reshape(g, tc * c, tc * c)
    wr = ab_re[::-1].transpose(1, 0, 3, 2).reshape(g, tc * c, p)
    wi = ab_im[::-1].transpose(1, 0, 3, 2).reshape(g, tc * c, p)
    w = jnp.concatenate([wr, wi], axis=-1)
    wsw = jnp.concatenate([wi, wr], axis=-1)
    e_re = c_re[None] * pr[1:, :, None, :] - c_im[None] * pi[1:, :, None, :]
    e_im = c_re[None] * pi[1:, :, None, :] + c_im[None] * pr[1:, :, None, :]
    v = jnp.concatenate([e_re.transpose(1, 3, 0, 2).reshape(g, p, tc * c),
                         -e_im.transpose(1, 3, 0, 2).reshape(g, p, tc * c)], axis=1)
    dd = jnp.tile(d_skip.reshape(g, 1, c), (1, tc, 1)).reshape(g, 1, tc * c)
    ar = jnp.concatenate([pr[tc], pr[tc]], axis=-1)[:, None, :]
    ai = jnp.concatenate([-pi[tc], pi[tc]], axis=-1)[:, None, :]
    return m.astype(BF16), w.astype(BF16), wsw.astype(BF16), v.astype(BF16), dd, ar, ai


def _s5_body(u_ref, m_ref, w_ref, wsw_ref, v_ref, d_ref, ar_ref, ai_ref, h0_ref, h0s_ref,
             y_ref, hout_ref, h_s, hs_s, hin_s, *, cb, nb):
    @pl.when(pl.program_id(1) == 0)
    def _():
        h_s[...] = h0_ref[0]
        hs_s[...] = h0s_ref[0]

    kd = u_ref.shape[-1]
    u = u_ref[0].reshape(cb * nb, kd)
    ub = u.astype(BF16)
    yi = _dot(ub, m_ref[0])
    s = _dot(ub, w_ref[0])
    ssw = _dot(ub, wsw_ref[0])
    ar = ar_ref[0]
    ai = ai_ref[0]
    h = h_s[...]
    hs = hs_s[...]
    for c in range(cb):
        lo, hi = c * nb, (c + 1) * nb
        hin_s[lo:hi, :] = h
        h, hs = ar * h + ai * hs + s[lo:hi], ar * hs - ai * h + ssw[lo:hi]
    h_s[...] = h
    hs_s[...] = hs
    hout_ref[0] = h
    yo = _dot(hin_s[...].astype(BF16), v_ref[0])
    y_ref[0] = (yi + yo + d_ref[0] * u).reshape(cb, nb, kd)


def _s5(u, h0_re, h0_im, prm, n, l, tc):
    m, w, wsw, v, dd, ar, ai = prm
    g = m.shape[0]
    c = SSM_GROUP
    p2 = 2 * SSM_STATE
    nc = l // tc
    kd = tc * c
    u2 = u.reshape(n, nc, tc, g, c).transpose(3, 1, 0, 2, 4).reshape(g, nc, n, kd)
    h0 = jnp.concatenate([h0_re, h0_im], axis=-1).transpose(1, 0, 2)
    h0s = jnp.concatenate([h0_im, h0_re], axis=-1).transpose(1, 0, 2)
    cb = max(1, min(nc, S5_ROWS // n))
    assert nc % cb == 0 and n % SUBLANES == 0
    par = lambda gi, j: (gi, 0, 0)
    blk = lambda gi, j: (gi, j, 0, 0)
    y2, hout = pl.pallas_call(
        functools.partial(_s5_body, cb=cb, nb=n),
        grid=(g, nc // cb),
        in_specs=[pl.BlockSpec((1, cb, n, kd), blk),
                  pl.BlockSpec((1, kd, kd), par), pl.BlockSpec((1, kd, p2), par), pl.BlockSpec((1, kd, p2), par),
                  pl.BlockSpec((1, p2, kd), par), pl.BlockSpec((1, 1, kd), par),
                  pl.BlockSpec((1, 1, p2), par), pl.BlockSpec((1, 1, p2), par),
                  pl.BlockSpec((1, n, p2), par), pl.BlockSpec((1, n, p2), par)],
        out_specs=[pl.BlockSpec((1, cb, n, kd), blk), pl.BlockSpec((1, n, p2), par)],
        out_shape=[jax.ShapeDtypeStruct((g, nc, n, kd), F32), jax.ShapeDtypeStruct((g, n, p2), F32)],
        scratch_shapes=[pltpu.VMEM((n, p2), F32), pltpu.VMEM((n, p2), F32), pltpu.VMEM((cb * n, p2), F32)],
        compiler_params=_cparams(("parallel", "arbitrary")),
    )(u2, m, w, wsw, v, dd, ar, ai, h0, h0s)
    y = y2.reshape(g, nc, n, tc, c).transpose(2, 1, 3, 0, 4).reshape(n * l, g * c)
    hout = hout.transpose(1, 0, 2)
    return y, hout[..., :SSM_STATE], hout[..., SSM_STATE:]


def _attn_body(lam_ref, q_ref, k_ref, v_ref, sw_ref, o_ref, q2_s, m_s, l_s, acc_s, *, tq, hd, post):
    qi = pl.program_id(2)
    q = q_ref[0]
    lane = lax.broadcasted_iota(jnp.int32, q.shape, 1)
    zero = jnp.zeros_like(q)
    q2_s[0:tq, :] = jnp.where(lane < hd, q, zero)
    q2_s[tq:, :] = jnp.where(lane >= hd, q, zero)
    m_s[...] = jnp.full(m_s.shape, NEG_BIG, F32)
    l_s[...] = jnp.zeros(l_s.shape, F32)
    acc_s[...] = jnp.zeros(acc_s.shape, F32)

    def step(j, masked):
        start = pl.multiple_of(j * tq, tq)
        kb = k_ref[0, pl.ds(start, tq), :]
        vb = v_ref[0, pl.ds(start, tq), :]
        s = _dot_nt(q2_s[...], kb)
        if masked:
            r = lax.broadcasted_iota(jnp.int32, s.shape, 0)
            c = lax.broadcasted_iota(jnp.int32, s.shape, 1)
            r = jnp.where(r >= tq, r - tq, r)
            s = jnp.where(c <= r, s, NEG_BIG)
        m_old = m_s[...]
        m_new = jnp.maximum(m_old, jnp.max(s, axis=-1, keepdims=True))
        alpha = jnp.exp(m_old - m_new)
        p = jnp.exp(s - m_new)
        l_s[...] = alpha * l_s[...] + jnp.sum(p, axis=-1, keepdims=True)
        acc_s[...] = alpha * acc_s[...] + _dot(p.astype(BF16), vb)
        m_s[...] = m_new

    def full_step(j, carry):
        step(j, False)
        return carry

    lax.fori_loop(0, qi, full_step, 0)
    step(qi, True)
    acc = acc_s[...]
    l = l_s[...]
    o = acc[:tq] / l[:tq] - lam_ref[0] * (acc[tq:] / l[tq:])
    o_ref[0] = (_rms(o, sw_ref[...]) * post).astype(o_ref.dtype)


def _attn_prompt(qb, kb, vb, lam, subln_w, n, l, post):
    aw = qb.shape[-1]
    vd = aw // ATT_HEADS
    tq = min(ATTN_BLOCK, l)
    q3, k3, v3 = (a.reshape(n, l, aw) for a in (qb, kb, vb))
    return pl.pallas_call(
        functools.partial(_attn_body, tq=tq, hd=vd // 2, post=post),
        grid=(n, ATT_HEADS, l // tq),
        in_specs=[pl.BlockSpec(memory_space=pltpu.SMEM),
                  pl.BlockSpec((1, tq, vd), lambda b, h, i: (b, i, h)),
                  pl.BlockSpec((1, l, vd), lambda b, h, i: (b, 0, h)),
                  pl.BlockSpec((1, l, vd), lambda b, h, i: (b, 0, h)),
                  pl.BlockSpec((1, vd), lambda b, h, i: (0, 0))],
        out_specs=pl.BlockSpec((1, tq, vd), lambda b, h, i: (b, i, h)),
        out_shape=jax.ShapeDtypeStruct((n, l, aw), BF16),
        scratch_shapes=[pltpu.VMEM((2 * tq, vd), BF16), pltpu.VMEM((2 * tq, 1), F32),
                        pltpu.VMEM((2 * tq, 1), F32), pltpu.VMEM((2 * tq, vd), F32)],
        compiler_params=_cparams(("parallel", "parallel", "arbitrary")),
    )(lam.reshape(1), q3, k3, v3, subln_w.reshape(1, vd)).reshape(n * l, aw)


def _attn_dec_body(pt_ref, lam_ref, qr_ref, kn_ref, vn_ref, sw_ref, *rest, pg, t_new, vd, post):
    k_refs, v_refs = rest[:pg], rest[pg:2 * pg]
    o_ref, m_s, l_s, acc_s = rest[2 * pg:]
    j = pl.program_id(1)

    @pl.when(j == 0)
    def _():
        m_s[...] = jnp.full(m_s.shape, NEG_BIG, F32)
        l_s[...] = jnp.zeros(l_s.shape, F32)
        acc_s[...] = jnp.zeros(acc_s.shape, F32)

    qr = qr_ref[0]

    def update(s, vs):
        m_old = m_s[...]
        m_new = jnp.maximum(m_old, jnp.max(s, axis=-1, keepdims=True))
        alpha = jnp.exp(m_old - m_new)
        p = jnp.exp(s - m_new).astype(BF16)
        l_s[...] = alpha * l_s[...] + jnp.sum(p.astype(F32), axis=-1, keepdims=True)
        w = vs[0].shape[0]
        pv = _dot(p[:, :w], vs[0])
        for i in range(1, len(vs)):
            pv = pv + _dot(p[:, i * w:(i + 1) * w], vs[i])
        acc_s[...] = alpha * acc_s[...] + pv
        m_s[...] = m_new

    s = jnp.concatenate([_dot_nt(qr, kr[0].astype(BF16)) for kr in k_refs], axis=1)
    update(s, [vr[0].astype(BF16) for vr in v_refs])

    @pl.when(j == pl.num_programs(1) - 1)
    def _():
        sn = _dot_nt(qr, kn_ref[0])
        r = lax.broadcasted_iota(jnp.int32, sn.shape, 0) % t_new
        c = lax.broadcasted_iota(jnp.int32, sn.shape, 1)
        update(jnp.where(c <= r, sn, NEG_BIG), [vn_ref[0]])
        acc = acc_s[...]
        l = l_s[...]
        rows = 2 * t_new
        for h in range(ATT_HEADS):
            blk = acc[h * rows:(h + 1) * rows, h * vd:(h + 1) * vd] / l[h * rows:(h + 1) * rows]
            o = blk[:t_new] - lam_ref[0] * blk[t_new:]
            o_ref[0, :, h * vd:(h + 1) * vd] = _rms(o, sw_ref[...]) * post


def _attn_sample(qb, kb, vb, cache_k, cache_v, page_table, lam, subln_w, nseq, t_new, post):
    aw = qb.shape[-1]
    vd = aw // ATT_HEADS
    hd = vd // 2
    n_pool, page = cache_k.shape[0], cache_k.shape[1]
    n_pages = page_table.shape[1]
    pg = math.gcd(DEC_PAGES, n_pages)
    ck = cache_k.reshape(n_pool, page, aw)
    cv = cache_v.reshape(n_pool, page, aw)
    q5 = qb.reshape(nseq, t_new, ATT_HEADS, 2, hd)
    qr = jnp.einsum('bqhcd,hx,cy->bhcqxyd', q5, jnp.eye(ATT_HEADS, dtype=BF16), jnp.eye(2, dtype=BF16))
    nr = ATT_HEADS * 2 * t_new
    qr = qr.reshape(nseq, nr, aw)
    pad = ((0, 0), (0, LANES - t_new), (0, 0))
    kn = jnp.pad(kb.reshape(nseq, t_new, aw), pad)
    vn = jnp.pad(vb.reshape(nseq, t_new, aw), pad)
    seq = lambda b, j, pt: (b, 0, 0)
    kv_specs = [pl.BlockSpec((1, page, aw), functools.partial(lambda b, j, pt, i: (pt[b, j * pg + i], 0, 0), i=i))
                for i in range(pg)]
    return pl.pallas_call(
        functools.partial(_attn_dec_body, pg=pg, t_new=t_new, vd=vd, post=post),
        grid_spec=pltpu.PrefetchScalarGridSpec(
            num_scalar_prefetch=1,
            grid=(nseq, n_pages // pg),
            in_specs=[pl.BlockSpec(memory_space=pltpu.SMEM),
                      pl.BlockSpec((1, nr, aw), seq), pl.BlockSpec((1, LANES, aw), seq),
                      pl.BlockSpec((1, LANES, aw), seq), pl.BlockSpec((1, vd), lambda b, j, pt: (0, 0))]
            + kv_specs + kv_specs,
            out_specs=pl.BlockSpec((1, t_new, aw), seq),
            scratch_shapes=[pltpu.VMEM((nr, 1), F32), pltpu.VMEM((nr, 1), F32), pltpu.VMEM((nr, aw), F32)]),
        out_shape=jax.ShapeDtypeStruct((nseq, t_new, aw), F32),
        compiler_params=_cparams(("parallel", "arbitrary")),
    )(page_table, lam.reshape(1), qr, kn, vn, subln_w.reshape(1, vd), *([ck] * pg), *([cv] * pg)
      ).reshape(nseq * t_new, aw)


def _mix_body(ys_ref, o_ref, x_ref, gw_ref, gb_ref, snw_ref, wo_ref, fnw_ref, rw_ref, rb_ref,
              h_ref, xn_ref, info_ref, infot_ref, *, sw):
    y = jax.nn.gelu(ys_ref[...])
    z = _dot(y.astype(BF16), gw_ref[...]) + gb_ref[...]
    y = y * (1.0 / (1.0 + jnp.exp(-z)))
    yn = _rms(y, snw_ref[...])
    mixed = _dot(yn.astype(BF16), wo_ref[:sw, :]) + _dot(o_ref[...].astype(BF16), wo_ref[sw:, :])
    h = x_ref[...] + mixed
    h_ref[...] = h
    xnb = _rms(h, fnw_ref[...]).astype(BF16)
    xn_ref[...] = xnb
    logits = _dot(xnb, rw_ref[...]) + rb_ref[...]
    lane = lax.broadcasted_iota(jnp.int32, logits.shape, 1).astype(F32)
    work = logits
    vals, idxs = [], []
    for _ in range(TOP_K):
        mx = jnp.max(work, axis=-1, keepdims=True)
        ix = jnp.min(jnp.where(work == mx, lane, float(LANES)), axis=-1, keepdims=True)
        vals.append(mx)
        idxs.append(ix)
        work = jnp.where(lane == ix, -3e38, work)
    es = [jnp.exp(v - vals[0]) for v in vals]
    den = es[0] + es[1] + es[2] + es[3]
    info = jnp.zeros(logits.shape, F32)
    for k in range(TOP_K):
        info = info + jnp.where(lane == float(k), idxs[k], 0.0) + jnp.where(lane == float(TOP_K + k), es[k] / den, 0.0)
    info_ref[...] = info[:, :2 * TOP_K]
    infot_ref[0] = info.T[:2 * TOP_K, :]


def _mix(ys, o, x, glu_w_b, glu_b, ssm_norm_w, w_out_b, ffn_norm_w, router_w_b, router_b_p):
    t, d = x.shape
    sw = ys.shape[1]
    aw = o.shape[1]
    tm = min(TOKEN_TILE, t)
    nt = t // tm
    row = lambda i: (i, 0)
    fix = lambda i: (0, 0)
    return pl.pallas_call(
        functools.partial(_mix_body, sw=sw),
        grid=(nt,),
        in_specs=[pl.BlockSpec((tm, sw), row), pl.BlockSpec((tm, aw), row), pl.BlockSpec((tm, d), row),
                  pl.BlockSpec((sw, sw), fix), pl.BlockSpec((1, sw), fix), pl.BlockSpec((1, sw), fix),
                  pl.BlockSpec((d, d), fix), pl.BlockSpec((1, d), fix),
                  pl.BlockSpec((d, LANES), fix), pl.BlockSpec((1, LANES), fix)],
        out_specs=[pl.BlockSpec((tm, d), row), pl.BlockSpec((tm, d), row),
                   pl.BlockSpec((tm, 2 * TOP_K), row), pl.BlockSpec((1, 2 * TOP_K, tm), lambda i: (i, 0, 0))],
        out_shape=[jax.ShapeDtypeStruct((t, d), F32), jax.ShapeDtypeStruct((t, d), BF16),
                   jax.ShapeDtypeStruct((t, 2 * TOP_K), F32), jax.ShapeDtypeStruct((nt, 2 * TOP_K, tm), F32)],
        compiler_params=_cparams(("parallel",)),
    )(ys, o, x, glu_w_b, glu_b.reshape(1, sw), ssm_norm_w.reshape(1, sw), w_out_b,
      ffn_norm_w.reshape(1, d), router_w_b, router_b_p)


def _seg_rows(tt, ne):
    worst = tt * TOP_K + ne * (UNIT - 1)
    return -(-worst // LANES) * LANES


def _moe_sort_body(x_ref, infot_ref, xs_ref, cp_ref, pos_ref, *, tt, ne, rt):
    infot = infot_ref[0]
    sub = lax.broadcasted_iota(jnp.int32, (ne, tt), 0).astype(F32)
    onehots = [(sub == infot[k:k + 1, :]).astype(F32) for k in range(TOP_K)]
    et = onehots[0] + onehots[1] + onehots[2] + onehots[3]
    cnt = jnp.sum(et, axis=1, keepdims=True)
    cp = jnp.floor((cnt + (UNIT - 1.0)) * (1.0 / UNIT))
    er = lax.broadcasted_iota(jnp.int32, (ne, ne), 0)
    ec = lax.broadcasted_iota(jnp.int32, (ne, ne), 1)
    cpb = jnp.broadcast_to(cp, (ne, LANES))
    lo = _dot((ec < er).astype(BF16), cpb.astype(BF16))[:, :1] * float(UNIT)
    tr = lax.broadcasted_iota(jnp.int32, (tt, tt), 0)
    tc = lax.broadcasted_iota(jnp.int32, (tt, tt), 1)
    rank = _dot(et.astype(BF16), (tr < tc).astype(BF16))
    base = lo + rank
    poss = [jnp.sum(oh * base, axis=0, keepdims=True) for oh in onehots]
    rows = lax.broadcasted_iota(jnp.int32, (rt, tt), 0).astype(F32)
    sel = (rows == poss[0]).astype(F32)
    for k in range(1, TOP_K):
        sel = sel + (rows == poss[k]).astype(F32)
    xs_ref[...] = _dot(sel.astype(BF16), x_ref[...])
    cp_ref[0] = cpb.astype(jnp.int32)
    pt = jnp.concatenate(poss + [jnp.zeros((LANES - TOP_K, tt), F32)], axis=0)
    pos_ref[...] = pt.T[:, :2 * TOP_K]


def _moe_sort(xn, infot, ne):
    t, d = xn.shape
    nt, _, tt = infot.shape
    rt = _seg_rows(tt, ne)
    return pl.pallas_call(
        functools.partial(_moe_sort_body, tt=tt, ne=ne, rt=rt),
        grid=(nt,),
        in_specs=[pl.BlockSpec((tt, d), lambda i: (i, 0)), pl.BlockSpec((1, 2 * TOP_K, tt), lambda i: (i, 0, 0))],
        out_specs=[pl.BlockSpec((rt, d), lambda i: (i, 0)), pl.BlockSpec((1, ne, LANES), lambda i: (i, 0, 0)),
                   pl.BlockSpec((tt, 2 * TOP_K), lambda i: (i, 0))],
        out_shape=[jax.ShapeDtypeStruct((nt * rt, d), F32), jax.ShapeDtypeStruct((nt, ne, LANES), jnp.int32),
                   jax.ShapeDtypeStruct((t, 2 * TOP_K), F32)],
        compiler_params=_cparams(("parallel",)),
    )(xn, infot)


def _moe_tables(cp, rt, bu):
    nt, ne = cp.shape
    ru = rt // UNIT
    lo = jnp.cumsum(cp, axis=1) - cp
    used = jnp.sum(cp, axis=1)
    seg_len = jnp.concatenate([cp.T, (ru - used)[None, :]], axis=0)
    seg_src = jnp.concatenate([(jnp.arange(nt)[:, None] * ru + lo).T,
                               (jnp.arange(nt) * ru + used)[None, :]], axis=0)
    tot = jnp.sum(seg_len, axis=1)
    totp = -(-tot // bu) * bu
    gend = jnp.cumsum(totp)
    gstart = gend - totp
    seg_start = (gstart[:, None] + jnp.cumsum(seg_len, axis=1) - seg_len).reshape(-1)
    nblk = (nt * ru) // bu + ne + 1
    slot = jnp.arange(nblk * bu, dtype=jnp.int32)
    e = jnp.searchsorted(gend, slot, side='right').astype(jnp.int32)
    ec = jnp.minimum(e, ne)
    is_pad = (e > ne) | (slot - gstart[ec] >= tot[ec])
    s = jnp.clip(jnp.searchsorted(seg_start, slot, side='right') - 1, 0, seg_start.shape[0] - 1)
    unit = (seg_src.reshape(-1)[s] + slot - seg_start[s]).astype(jnp.int32)
    dump = nt * ru + slot % bu
    tab = jnp.concatenate([jnp.where(is_pad, 0, unit).reshape(nblk, bu),
                           jnp.where(is_pad, dump, unit).reshape(nblk, bu)], axis=1)
    pad_l = -(-2 * bu // LANES) * LANES - 2 * bu
    tab = jnp.pad(tab, ((0, 0), (0, pad_l))).reshape(nblk, 1, -1).astype(jnp.int32)
    bexp = e[::bu]
    nact = (gend[ne] // bu).astype(jnp.int32).reshape(1)
    return tab, bexp, nact, nblk


def _moe_ffn_body(bexp_ref, nact_ref, tab_ref, tabn_ref, xs_hbm, wup_ref, bup_ref, wdn_ref, bdn_ref,
                  out_hbm, xbuf, obuf, sem_in, sem_out, *, bu, ne):
    b = pl.program_id(0)
    nact = nact_ref[0]
    slot = b % 2
    d = xbuf.shape[-1]

    def start_in(tref, s):
        for j in range(bu):
            pltpu.make_async_copy(xs_hbm.at[tref[0, 0, j]], xbuf.at[s, j], sem_in.at[s]).start()

    def wait_in(s):
        for j in range(bu):
            pltpu.make_async_copy(xs_hbm.at[0], xbuf.at[s, j], sem_in.at[s]).wait()

    def start_out():
        for j in range(bu):
            pltpu.make_async_copy(obuf.at[j], out_hbm.at[tab_ref[0, 0, bu + j]], sem_out).start()

    def wait_out():
        for j in range(bu):
            pltpu.make_async_copy(obuf.at[j], out_hbm.at[0], sem_out).wait()

    @pl.when(b < nact)
    def _():
        @pl.when(b == 0)
        def _():
            start_in(tab_ref, 0)

        wait_in(slot)

        @pl.when(b + 1 < nact)
        def _():
            start_in(tabn_ref, 1 - slot)

        e = bexp_ref[b]

        @pl.when(e < ne)
        def _():
            x = xbuf[slot].reshape(bu * UNIT, d).astype(BF16)
            h = _dot(x, wup_ref[0]) + bup_ref[0]
            f = h.shape[1] // 2
            glu = jnp.minimum(h[:, :f], SWIGLU_LIMIT)
            lin = jnp.clip(h[:, f:], -SWIGLU_LIMIT, SWIGLU_LIMIT)
            act = glu * (1.0 / (1.0 + jnp.exp(-SWIGLU_ALPHA * glu))) * (lin + 1.0)
            out = _dot(act.astype(BF16), wdn_ref[0]) + bdn_ref[0]

            @pl.when(b > 0)
            def _():
                wait_out()

            obuf[...] = out.reshape(bu, UNIT, d)

        @pl.when(e >= ne)
        def _():
            @pl.when(b > 0)
            def _():
                wait_out()

            obuf[...] = jnp.zeros(obuf.shape, F32)

        start_out()

        @pl.when(b == nact - 1)
        def _():
            wait_out()
            nu = out_hbm.shape[0] - bu
            for j in range(bu):
                pltpu.make_async_copy(obuf.at[j], out_hbm.at[nu + j], sem_out).start()
            wait_out()


def _moe_ffn(xs, cp, wup_b, bup, wdn_b, bdn, rt):
    ne, d, f2 = wup_b.shape
    bu = FFN_ROWS // UNIT
    nt = cp.shape[0]
    tab, bexp, nact, nblk = _moe_tables(cp, rt, bu)
    nu = xs.shape[0] // UNIT
    xs3 = xs.reshape(nu, UNIT, d)
    tl = tab.shape[-1]
    wsel = lambda b, be, na: (jnp.minimum(be[b], ne - 1), 0, 0)
    out = pl.pallas_call(
        functools.partial(_moe_ffn_body, bu=bu, ne=ne),
        grid_spec=pltpu.PrefetchScalarGridSpec(
            num_scalar_prefetch=2,
            grid=(nblk,),
            in_specs=[pl.BlockSpec((1, 1, tl), lambda b, be, na: (b, 0, 0), memory_space=pltpu.SMEM),
                      pl.BlockSpec((1, 1, tl), lambda b, be, na: (jnp.minimum(b + 1, nblk - 1), 0, 0),
                                   memory_space=pltpu.SMEM),
                      pl.BlockSpec(memory_space=pl.ANY),
                      pl.BlockSpec((1, d, f2), wsel), pl.BlockSpec((1, 1, f2), wsel),
                      pl.BlockSpec((1, f2 // 2, d), wsel), pl.BlockSpec((1, 1, d), wsel)],
            out_specs=pl.BlockSpec(memory_space=pl.ANY),
            scratch_shapes=[pltpu.VMEM((2, bu, UNIT, d), F32), pltpu.VMEM((bu, UNIT, d), F32),
                            pltpu.SemaphoreType.DMA((2,)), pltpu.SemaphoreType.DMA(())]),
        out_shape=jax.ShapeDtypeStruct((nu + bu, UNIT, d), F32),
        compiler_params=_cparams(("arbitrary",)),
    )(bexp, nact, tab, tab, xs3, wup_b, bup.reshape(ne, 1, f2), wdn_b, bdn.reshape(ne, 1, d))
    return out.reshape((nu + bu) * UNIT, d)


def _moe_combine_body(o_ref, pos_ref, info_ref, h_ref, fw_ref, y_ref, *, rt):
    pos = pos_ref[...]
    info = info_ref[...]
    lane = lax.broadcasted_iota(jnp.int32, (pos.shape[0], rt), 1).astype(F32)
    selt = jnp.where(lane == pos[:, 0:1], info[:, TOP_K:TOP_K + 1], 0.0)
    for k in range(1, TOP_K):
        selt = selt + jnp.where(lane == pos[:, k:k + 1], info[:, TOP_K + k:TOP_K + k + 1], 0.0)
    y = _dot(selt.astype(BF16), o_ref[...].astype(BF16))
    y_ref[...] = _rms(h_ref[...] + y, fw_ref[...])


def _moe_combine(outs, pos, info, h, final_w, rt):
    t, d = h.shape
    tt = min(TOKEN_TILE, t)
    row = lambda i: (i, 0)
    return pl.pallas_call(
        functools.partial(_moe_combine_body, rt=rt),
        grid=(t // tt,),
        in_specs=[pl.BlockSpec((rt, d), row), pl.BlockSpec((tt, 2 * TOP_K), row), pl.BlockSpec((tt, 2 * TOP_K), row),
                  pl.BlockSpec((tt, d), row), pl.BlockSpec((1, d), lambda i: (0, 0))],
        out_specs=pl.BlockSpec((tt, d), row),
        out_shape=jax.ShapeDtypeStruct((t, d), F32),
        compiler_params=_cparams(("parallel",)),
    )(outs, pos, info, h, final_w.reshape(1, d))


def _group(x3, h0_re, h0_im, past, tc, wts):
    n, l, d = x3.shape
    x = x3.reshape(n * l, d)
    sw, aw = wts['sw'], wts['aw']
    hd = aw // (2 * ATT_HEADS)
    u, k, v, qb, kb, vb = _inproj(x, wts['attn_norm_w'], wts['w_in'], sw, aw, hd ** -0.5)
    ys, st_re, st_im = _s5(u, h0_re, h0_im, wts['s5'][tc], n, l, tc)
    if past is None:
        o = _attn_prompt(qb, kb, vb, wts['lam'], wts['subln_w'], n, l, wts['post'])
    else:
        o = _attn_sample(qb, kb, vb, past[0], past[1], past[2], wts['lam'], wts['subln_w'], n, l, wts['post'])
    h, xn, info, infot = _mix(ys, o, x, wts['glu_w'], wts['glu_b'], wts['ssm_norm_w'], wts['w_out'],
                              wts['ffn_norm_w'], wts['router_w'], wts['router_b'])
    ne = wts['w_up'].shape[0]
    rt = _seg_rows(infot.shape[2], ne)
    xs, cpl, pos = _moe_sort(xn, infot, ne)
    outs = _moe_ffn(xs, cpl[:, :, 0], wts['w_up'], wts['b_up'], wts['w_down'], wts['b_down'], rt)
    y = _moe_combine(outs, pos, info, h, wts['final_norm_w'], rt)
    heads = (n, l, ATT_HEADS)
    return (y.reshape(n, l, d), k.reshape(heads + (2, hd))[None], v.reshape(heads + (2 * hd,))[None],
            st_re[None], st_im[None])


def kernel(x_prompt, x_sample, cache_k, cache_v, state_ssm_re, state_ssm_im, page_table, attn_norm_w, w_in, ssm_lambda_re, ssm_lambda_im, ssm_log_dt, ssm_b_re, ssm_b_im, ssm_c_re, ssm_c_im, ssm_d, ssm_glu_w, ssm_glu_b, ssm_norm_w, diff_lambda_q1, diff_lambda_k1, diff_lambda_q2, diff_lambda_k2, subln_w, w_out, ffn_norm_w, router_w, router_b, w_up, b_up, w_down, b_down, final_norm_w):
    assert w_in.shape[0] == 1, "single-layer trunk"
    d = x_prompt.shape[-1]
    g = ssm_lambda_re.shape[1]
    sw = g * SSM_GROUP
    aw = (w_in.shape[-1] - sw) // 3
    ne = router_w.shape[-1]
    f = w_down.shape[2]
    lam_init = 0.8 - 0.6 * math.exp(-0.3 * 0)
    lam = (jnp.exp(jnp.sum(diff_lambda_q1[0] * diff_lambda_k1[0]))
           - jnp.exp(jnp.sum(diff_lambda_q2[0] * diff_lambda_k2[0])) + lam_init).astype(F32)
    t_dec = x_sample.shape[1]
    tc_p = min(S5_CHUNK, x_prompt.shape[1])
    s5_args = (ssm_lambda_re[0], ssm_lambda_im[0], ssm_log_dt[0], ssm_b_re[0], ssm_b_im[0],
               ssm_c_re[0], ssm_c_im[0], ssm_d[0])
    wts = {
        'sw': sw, 'aw': aw, 'lam': lam, 'post': 1.0 - lam_init,
        'attn_norm_w': attn_norm_w[0], 'w_in': w_in[0].astype(BF16),
        's5': {tc: _s5_params(*s5_args, tc) for tc in {tc_p, t_dec}},
        'glu_w': ssm_glu_w[0].astype(BF16), 'glu_b': ssm_glu_b[0], 'ssm_norm_w': ssm_norm_w[0],
        'subln_w': subln_w[0], 'w_out': w_out[0].astype(BF16), 'ffn_norm_w': ffn_norm_w[0],
        'router_w': jnp.pad(router_w[0], ((0, 0), (0, LANES - ne))).astype(BF16),
        'router_b': jnp.pad(router_b[0], (0, LANES - ne), constant_values=NEG_BIG).reshape(1, LANES),
        'w_up': jnp.concatenate([w_up[0, :, :, 0::2], w_up[0, :, :, 1::2]], axis=-1).astype(BF16),
        'b_up': jnp.concatenate([b_up[0, :, 0::2], b_up[0, :, 1::2]], axis=-1),
        'w_down': w_down[0].astype(BF16), 'b_down': b_down[0], 'final_norm_w': final_norm_w,
    }
    nb = x_prompt.shape[0]
    zeros = jnp.zeros((nb, g, SSM_STATE), F32)
    yp, kp, vp, srp, sip = _group(x_prompt, zeros, zeros, None, tc_p, wts)
    ys, ks, vs, srs, sis = _group(x_sample, state_ssm_re[0], state_ssm_im[0],
                                  (cache_k[0], cache_v[0], page_table), t_dec, wts)
    return (yp, ys, kp, vp, srp, sip, ks, vs, srs, sis)
```

```python
import functools
import math

import jax
import jax.numpy as jnp
from jax import lax
from jax.experimental import pallas as pl
from jax.experimental.pallas import tpu as pltpu

F32 = jnp.float32
BF16 = jnp.bfloat16

SSM_GROUP = 16
SSM_STATE = 64
ATT_HEADS = 4
TOP_K = 4
SWIGLU_LIMIT = 7.0
SWIGLU_ALPHA = 1.702
RMS_EPS = 1e-6
NEG_BIG = -1e30

LANES = 128
SUBLANES = 8
VMEM_LIMIT = 56 * 1024 * 1024

INPROJ_ROWS = 512
ATTN_Q = 256
ATTN_K = 512
S5_CHUNK = 16
S5_ROWS = 512
TOKEN_TILE = 256
FFN_ROWS = 512
UNIT = SUBLANES
DEC_PAGES = 8


def _cparams(sem):
    return pltpu.CompilerParams(dimension_semantics=sem, vmem_limit_bytes=VMEM_LIMIT)


def _rms(x, w):
    ms = jnp.mean(x * x, axis=-1, keepdims=True)
    return x * lax.rsqrt(ms + RMS_EPS) * w


def _dot(a, b):
    return jnp.dot(a, b, preferred_element_type=F32)


def _dot_nt(a, b):
    return lax.dot_general(a, b, (((1,), (1,)), ((), ())), preferred_element_type=F32)


def _lane_mask(shape, lo, width):
    lane = lax.broadcasted_iota(jnp.int32, shape, 1)
    return (lane >= lo) & (lane < lo + width)


def _inproj_body(x_ref, nw_ref, w_ref, *refs, sw, aw, scale, pack):
    xn = _rms(x_ref[...], nw_ref[...])
    proj = _dot(xn.astype(BF16), w_ref[...])
    q = proj[:, sw:sw + aw]
    k = proj[:, sw + aw:sw + 2 * aw]
    v = proj[:, sw + 2 * aw:]
    if not pack:
        u_ref, k_ref, v_ref, qb_ref, kb_ref, vb_ref = refs
        vb_ref[...] = v.astype(BF16)
    else:
        u_ref, k_ref, v_ref, qb_ref, kb_ref, vt_ref, u2_ref, u_s = refs
        for vt in range(sw // LANES):
            u_s[vt] = proj[:, vt * LANES:(vt + 1) * LANES]
        vt_ref[0] = v.T.astype(BF16)
    u_ref[...] = proj[:, :sw]
    k_ref[...] = k
    v_ref[...] = v
    qb_ref[...] = (q * scale).astype(BF16)
    kb_ref[...] = k.astype(BF16)
    if pack:
        nk = u_s.shape[1] // S5_CHUNK
        per = LANES // SSM_GROUP
        for vt in range(sw // LANES):
            for half in range(S5_CHUNK // per):
                accs = [None] * per
                for s8 in range(per):
                    xv = u_s[vt, pl.ds(half * per + s8, nk, stride=S5_CHUNK), :]
                    mask = _lane_mask(xv.shape, s8 * SSM_GROUP, SSM_GROUP)
                    for gp in range(per):
                        shift = (SSM_GROUP * (s8 - gp)) % LANES
                        r = pltpu.roll(xv, shift, axis=1) if shift else xv
                        accs[gp] = r if s8 == 0 else jnp.where(mask, r, accs[gp])
                for gp in range(per):
                    u2_ref[vt * per + gp, :, half * LANES:(half + 1) * LANES] = accs[gp].astype(BF16)


def _inproj(x, norm_w, w_in_b, sw, aw, scale, pack):
    t, d = x.shape
    tm = min(INPROJ_ROWS, t)
    row = lambda i: (i, 0)
    fix = lambda i: (0, 0)
    f32o = lambda w: jax.ShapeDtypeStruct((t, w), F32)
    b16o = lambda w: jax.ShapeDtypeStruct((t, w), BF16)
    out_specs = [pl.BlockSpec((tm, sw), row)] + [pl.BlockSpec((tm, aw), row)] * 4
    out_shape = [f32o(sw), f32o(aw), f32o(aw), b16o(aw), b16o(aw)]
    scratch = []
    if pack:
        g = sw // SSM_GROUP
        kd = S5_CHUNK * SSM_GROUP
        out_specs += [pl.BlockSpec((1, aw, tm), lambda i: (i, 0, 0)),
                      pl.BlockSpec((g, tm // S5_CHUNK, kd), lambda i: (0, i, 0))]
        out_shape += [jax.ShapeDtypeStruct((t // tm, aw, tm), BF16),
                      jax.ShapeDtypeStruct((g, t // S5_CHUNK, kd), BF16)]
        scratch = [pltpu.VMEM((sw // LANES, tm, LANES), F32)]
    else:
        out_specs += [pl.BlockSpec((tm, aw), row)]
        out_shape += [b16o(aw)]
    return pl.pallas_call(
        functools.partial(_inproj_body, sw=sw, aw=aw, scale=scale, pack=pack),
        grid=(t // tm,),
        in_specs=[pl.BlockSpec((tm, d), row), pl.BlockSpec((1, d), fix), pl.BlockSpec(w_in_b.shape, fix)],
        out_specs=out_specs, out_shape=out_shape, scratch_shapes=scratch,
        compiler_params=_cparams(("parallel",)),
    )(x, norm_w.reshape(1, d), w_in_b)


def _s5_params(lam_re, lam_im, log_dt, b_re, b_im, c_re, c_im, tc):
    hp = lax.Precision.HIGHEST
    g, p = lam_re.shape
    c = b_re.shape[-1]
    dt = jnp.exp(log_dt)[:, None]
    mag = jnp.exp(lam_re * dt)
    a_re = mag * jnp.cos(lam_im * dt)
    a_im = mag * jnp.sin(lam_im * dt)
    den = lam_re * lam_re + lam_im * lam_im
    nr = a_re - 1.0
    cf_re = (nr * lam_re + a_im * lam_im) / den
    cf_im = (a_im * lam_re - nr * lam_im) / den
    bb_re = cf_re[..., None] * b_re - cf_im[..., None] * b_im
    bb_im = cf_re[..., None] * b_im + cf_im[..., None] * b_re
    pr, pi = [jnp.ones_like(a_re)], [jnp.zeros_like(a_re)]
    for _ in range(tc):
        pr, pi = pr + [pr[-1] * a_re - pi[-1] * a_im], pi + [pr[-1] * a_im + pi[-1] * a_re]
    pr = jnp.stack(pr)
    pi = jnp.stack(pi)
    ab_re = pr[:tc, :, :, None] * bb_re - pi[:tc, :, :, None] * bb_im
    ab_im = pr[:tc, :, :, None] * bb_im + pi[:tc, :, :, None] * bb_re
    kern = (jnp.einsum('gdp,tgpc->tgdc', c_re, ab_re, precision=hp)
            - jnp.einsum('gdp,tgpc->tgdc', c_im, ab_im, precision=hp))
    lag = jnp.arange(tc)[None, :] - jnp.arange(tc)[:, None]
    kt = kern[jnp.clip(lag, 0, tc - 1)]
    kt = jnp.where((lag >= 0)[:, :, None, None, None], kt, 0.0)
    m = kt.transpose(2, 0, 4, 1, 3).reshape(g, tc * c, tc * c)
    wr = ab_re[::-1].transpose(1, 0, 3, 2).reshape(g, tc * c, p)
    wi = ab_im[::-1].transpose(1, 0, 3, 2).reshape(g, tc * c, p)
    w = jnp.concatenate([wr, wi], axis=-1)
    wsw = jnp.concatenate([wi, wr], axis=-1)
    e_re = c_re[None] * pr[1:, :, None, :] - c_im[None] * pi[1:, :, None, :]
    e_im = c_re[None] * pi[1:, :, None, :] + c_im[None] * pr[1:, :, None, :]
    v = jnp.concatenate([e_re.transpose(1, 3, 0, 2).reshape(g, p, tc * c),
                         -e_im.transpose(1, 3, 0, 2).reshape(g, p, tc * c)], axis=1)
    ar = jnp.concatenate([pr[tc], pr[tc]], axis=-1)[:, None, :]
    ai = jnp.concatenate([-pi[tc], pi[tc]], axis=-1)[:, None, :]
    return m.astype(BF16), w.astype(BF16), wsw.astype(BF16), v.astype(BF16), ar, ai


def _s5_body(u_ref, m_ref, w_ref, wsw_ref, v_ref, ar_ref, ai_ref, h0_ref, h0s_ref,
             y_ref, hout_ref, h_s, hs_s, s_s, ssw_s, hin_s, *, cb, nb, seq_major):
    @pl.when(pl.program_id(1) == 0)
    def _():
        h_s[...] = h0_ref[0]
        hs_s[...] = h0s_ref[0]

    kd = u_ref.shape[-1]
    ub = u_ref[0].reshape(cb * nb, kd).astype(BF16)
    yi = _dot(ub, m_ref[0])
    s_s[...] = _dot(ub, w_ref[0])
    ssw_s[...] = _dot(ub, wsw_ref[0])
    ar = ar_ref[0]
    ai = ai_ref[0]
    h = h_s[...]
    hs = hs_s[...]
    for c in range(cb):
        idx = pl.ds(c, nb, stride=cb) if (seq_major and cb > 1) else pl.ds(c * nb, nb)
        hin_s[idx, :] = h
        h, hs = ar * h + ai * hs + s_s[idx, :], ar * hs - ai * h + ssw_s[idx, :]
    h_s[...] = h
    hs_s[...] = hs
    hout_ref[0] = h
    yo = _dot(hin_s[...].astype(BF16), v_ref[0])
    y_ref[0] = (yi + yo).reshape(y_ref.shape[1:])


def _s5(u_in, h0_re, h0_im, prm, n, l, tc, packed):
    m, w, wsw, v, ar, ai = prm
    g = m.shape[0]
    c = SSM_GROUP
    p2 = 2 * SSM_STATE
    nc = l // tc
    kd = tc * c
    cb = max(1, min(nc, S5_ROWS // n))
    assert nc % cb == 0 and n % SUBLANES == 0
    if packed:
        u2 = u_in.reshape(g, n, nc, kd)
        blk_shape, blk = (1, n, cb, kd), (lambda gi, j: (gi, 0, j, 0))
    else:
        u2 = u_in.reshape(n, nc, tc, g, c).transpose(3, 1, 0, 2, 4).reshape(g, nc, n, kd).astype(BF16)
        blk_shape, blk = (1, cb, n, kd), (lambda gi, j: (gi, j, 0, 0))
    h0 = jnp.concatenate([h0_re, h0_im], axis=-1).transpose(1, 0, 2)
    h0s = jnp.concatenate([h0_im, h0_re], axis=-1).transpose(1, 0, 2)
    par = lambda gi, j: (gi, 0, 0)
    y2, hout = pl.pallas_call(
        functools.partial(_s5_body, cb=cb, nb=n, seq_major=packed),
        grid=(g, nc // cb),
        in_specs=[pl.BlockSpec(blk_shape, blk),
                  pl.BlockSpec((1, kd, kd), par), pl.BlockSpec((1, kd, p2), par), pl.BlockSpec((1, kd, p2), par),
                  pl.BlockSpec((1, p2, kd), par),
                  pl.BlockSpec((1, 1, p2), par), pl.BlockSpec((1, 1, p2), par),
                  pl.BlockSpec((1, n, p2), par), pl.BlockSpec((1, n, p2), par)],
        out_specs=[pl.BlockSpec(blk_shape, blk), pl.BlockSpec((1, n, p2), par)],
        out_shape=[jax.ShapeDtypeStruct(u2.shape, F32), jax.ShapeDtypeStruct((g, n, p2), F32)],
        scratch_shapes=[pltpu.VMEM((n, p2), F32), pltpu.VMEM((n, p2), F32), pltpu.VMEM((cb * n, p2), F32),
                        pltpu.VMEM((cb * n, p2), F32), pltpu.VMEM((cb * n, p2), F32)],
        compiler_params=_cparams(("parallel", "arbitrary")),
    )(u2, m, w, wsw, v, ar, ai, h0, h0s)
    if packed:
        y = y2.reshape(g, n * nc, kd)
    else:
        y = y2.reshape(g, nc, n, tc, c).transpose(2, 1, 3, 0, 4).reshape(n * l, g * c)
    hout = hout.transpose(1, 0, 2)
    return y, hout[..., :SSM_STATE], hout[..., SSM_STATE:]


def _attn_body(lam_ref, q_ref, k_ref, vt_ref, swb_ref, o_ref, q2_s, m_s, acc_s, *, tq, tk, vd, post):
    qi = pl.program_id(2)
    hd = vd // 2
    q = q_ref[0]
    lane = lax.broadcasted_iota(jnp.int32, q.shape, 1)
    zero = jnp.zeros_like(q)
    q2_s[0:tq, :] = jnp.where(lane < hd, q, zero)
    q2_s[tq:, :] = jnp.where(lane >= hd, q, zero)
    m_s[...] = jnp.full(m_s.shape, NEG_BIG, F32)
    acc_s[...] = jnp.zeros(acc_s.shape, F32)
    ones = jnp.ones((vd, tk), BF16)

    def step(j, off):
        start = pl.multiple_of(j * tk, tk)
        kb = k_ref[0, pl.ds(start, tk), :]
        vaug = jnp.concatenate([vt_ref[j], ones], axis=0)
        for c in range(2):
            cs = slice(c * tq, (c + 1) * tq)
            st = _dot_nt(kb, q2_s[cs, :])
            if off is not None:
                key = lax.broadcasted_iota(jnp.int32, st.shape, 0)
                row = lax.broadcasted_iota(jnp.int32, st.shape, 1)
                st = jnp.where(key <= row + off, st, NEG_BIG)
            m_old = m_s[:, cs]
            m_new = jnp.maximum(m_old, jnp.max(st, axis=0, keepdims=True))
            alpha = jnp.exp(m_old - m_new)
            p = jnp.exp(st - m_new).astype(BF16)
            acc_s[:, cs] = alpha * acc_s[:, cs] + _dot(vaug, p)
            m_s[:, cs] = m_new

    def full_step(j, carry):
        step(j, None)
        return carry

    nfull = (qi * tq) // tk
    lax.fori_loop(0, nfull, full_step, 0)
    step(nfull, qi * tq - nfull * tk)
    acc = acc_s[...]
    ot = (acc[:vd, :tq] / acc[vd:vd + 1, :tq]) - lam_ref[0] * (acc[:vd, tq:] / acc[vd:vd + 1, tq:])
    ms = jnp.mean(ot * ot, axis=0, keepdims=True)
    ot = ot * lax.rsqrt(ms + RMS_EPS) * swb_ref[...] * post
    o_ref[0] = ot.T.astype(o_ref.dtype)


def _attn_prompt(qb, kb, vt, lam, subln_w, n, l, post):
    aw = qb.shape[-1]
    vd = aw // ATT_HEADS
    tk = vt.shape[-1]
    tq = min(ATTN_Q, l)
    assert l % tk == 0 and tk % tq == 0
    q3, k3 = qb.reshape(n, l, aw), kb.reshape(n, l, aw)
    swb = jnp.broadcast_to(subln_w.reshape(vd, 1), (vd, tq))
    return pl.pallas_call(
        functools.partial(_attn_body, tq=tq, tk=tk, vd=vd, post=post),
        grid=(n, ATT_HEADS, l // tq),
        in_specs=[pl.BlockSpec(memory_space=pltpu.SMEM),
                  pl.BlockSpec((1, tq, vd), lambda b, h, i: (b, i, h)),
                  pl.BlockSpec((1, l, vd), lambda b, h, i: (b, 0, h)),
                  pl.BlockSpec((l // tk, vd, tk), lambda b, h, i: (b, h, 0)),
                  pl.BlockSpec((vd, tq), lambda b, h, i: (0, 0))],
        out_specs=pl.BlockSpec((1, tq, vd), lambda b, h, i: (b, i, h)),
        out_shape=jax.ShapeDtypeStruct((n, l, aw), BF16),
        scratch_shapes=[pltpu.VMEM((2 * tq, vd), BF16), pltpu.VMEM((1, 2 * tq), F32),
                        pltpu.VMEM((2 * vd, 2 * tq), F32)],
        compiler_params=_cparams(("parallel", "parallel", "arbitrary")),
    )(lam.reshape(1), q3, k3, vt, swb).reshape(n * l, aw)


def _attn_dec_body(pt_ref, lam_ref, qr_ref, kn_ref, vn_ref, sw_ref, *rest, pg, t_new, vd, post):
    k_refs, v_refs = rest[:pg], rest[pg:2 * pg]
    o_ref, m_s, l_s, acc_s = rest[2 * pg:]
    j = pl.program_id(1)

    @pl.when(j == 0)
    def _():
        m_s[...] = jnp.full(m_s.shape, NEG_BIG, F32)
        l_s[...] = jnp.zeros(l_s.shape, F32)
        acc_s[...] = jnp.zeros(acc_s.shape, F32)

    qr = qr_ref[0]

    def update(s, vs):
        m_old = m_s[...]
        m_new = jnp.maximum(m_old, jnp.max(s, axis=-1, keepdims=True))
        alpha = jnp.exp(m_old - m_new)
        p = jnp.exp(s - m_new).astype(BF16)
        l_s[...] = alpha * l_s[...] + jnp.sum(p.astype(F32), axis=-1, keepdims=True)
        w = vs[0].shape[0]
        pv = _dot(p[:, :w], vs[0])
        for i in range(1, len(vs)):
            pv = pv + _dot(p[:, i * w:(i + 1) * w], vs[i])
        acc_s[...] = alpha * acc_s[...] + pv
        m_s[...] = m_new

    s = jnp.concatenate([_dot_nt(qr, kr[0].astype(BF16)) for kr in k_refs], axis=1)
    update(s, [vr[0].astype(BF16) for vr in v_refs])

    @pl.when(j == pl.num_programs(1) - 1)
    def _():
        sn = _dot_nt(qr, kn_ref[0])
        r = lax.broadcasted_iota(jnp.int32, sn.shape, 0) % t_new
        c = lax.broadcasted_iota(jnp.int32, sn.shape, 1)
        update(jnp.where(c <= r, sn, NEG_BIG), [vn_ref[0]])
        acc = acc_s[...]
        l = l_s[...]
        rows = 2 * t_new
        for h in range(ATT_HEADS):
            blk = acc[h * rows:(h + 1) * rows, h * vd:(h + 1) * vd] / l[h * rows:(h + 1) * rows]
            o = blk[:t_new] - lam_ref[0] * blk[t_new:]
            o_ref[0, :, h * vd:(h + 1) * vd] = _rms(o, sw_ref[...]) * post


def _attn_sample(qb, kb, vb, cache_k, cache_v, page_table, lam, subln_w, nseq, t_new, post):
    aw = qb.shape[-1]
    vd = aw // ATT_HEADS
    hd = vd // 2
    n_pool, page = cache_k.shape[0], cache_k.shape[1]
    n_pages = page_table.shape[1]
    pg = math.gcd(DEC_PAGES, n_pages)
    ck = cache_k.reshape(n_pool, page, aw)
    cv = cache_v.reshape(n_pool, page, aw)
    q5 = qb.reshape(nseq, t_new, ATT_HEADS, 2, hd)
    qr = jnp.einsum('bqhcd,hx,cy->bhcqxyd', q5, jnp.eye(ATT_HEADS, dtype=BF16), jnp.eye(2, dtype=BF16))
    nr = ATT_HEADS * 2 * t_new
    qr = qr.reshape(nseq, nr, aw)
    pad = ((0, 0), (0, LANES - t_new), (0, 0))
    kn = jnp.pad(kb.reshape(nseq, t_new, aw), pad)
    vn = jnp.pad(vb.reshape(nseq, t_new, aw), pad)
    seq = lambda b, j, pt: (b, 0, 0)
    kv_specs = [pl.BlockSpec((1, page, aw), functools.partial(lambda b, j, pt, i: (pt[b, j * pg + i], 0, 0), i=i))
                for i in range(pg)]
    return pl.pallas_call(
        functools.partial(_attn_dec_body, pg=pg, t_new=t_new, vd=vd, post=post),
        grid_spec=pltpu.PrefetchScalarGridSpec(
            num_scalar_prefetch=1,
            grid=(nseq, n_pages // pg),
            in_specs=[pl.BlockSpec(memory_space=pltpu.SMEM),
                      pl.BlockSpec((1, nr, aw), seq), pl.BlockSpec((1, LANES, aw), seq),
                      pl.BlockSpec((1, LANES, aw), seq), pl.BlockSpec((1, vd), lambda b, j, pt: (0, 0))]
            + kv_specs + kv_specs,
            out_specs=pl.BlockSpec((1, t_new, aw), seq),
            scratch_shapes=[pltpu.VMEM((nr, 1), F32), pltpu.VMEM((nr, 1), F32), pltpu.VMEM((nr, aw), F32)]),
        out_shape=jax.ShapeDtypeStruct((nseq, t_new, aw), F32),
        compiler_params=_cparams(("parallel", "arbitrary")),
    )(page_table, lam.reshape(1), qr, kn, vn, subln_w.reshape(1, vd), *([ck] * pg), *([cv] * pg)
      ).reshape(nseq * t_new, aw)


def _mix_body(ys_ref, u_ref, dsk_ref, o_ref, x_ref, gw_ref, gb_ref, snw_ref, wo_ref, fnw_ref, rw_ref, rb_ref,
              h_ref, xn_ref, info_ref, infot_ref, *scratch, sw, packed):
    if packed:
        ys_s, = scratch
        nk = ys_s.shape[1] // S5_CHUNK
        per = LANES // SSM_GROUP
        for vt in range(sw // LANES):
            for half in range(S5_CHUNK // per):
                srcs = [ys_ref[vt * per + gp, :, half * LANES:(half + 1) * LANES] for gp in range(per)]
                masks = [_lane_mask(srcs[0].shape, gp * SSM_GROUP, SSM_GROUP) for gp in range(per)]
                for s8 in range(per):
                    val = None
                    for gp in range(per):
                        shift = (SSM_GROUP * (gp - s8)) % LANES
                        r = pltpu.roll(srcs[gp], shift, axis=1) if shift else srcs[gp]
                        val = r if gp == 0 else jnp.where(masks[gp], r, val)
                    ys_s[vt, pl.ds(half * per + s8, nk, stride=S5_CHUNK), :] = val
        ys = jnp.concatenate([ys_s[vt] for vt in range(sw // LANES)], axis=1)
    else:
        ys = ys_ref[...]
    y = jax.nn.gelu(ys + dsk_ref[...] * u_ref[...])
    z = _dot(y.astype(BF16), gw_ref[...]) + gb_ref[...]
    y = y * (1.0 / (1.0 + jnp.exp(-z)))
    yn = _rms(y, snw_ref[...])
    mixed = _dot(yn.astype(BF16), wo_ref[:sw, :]) + _dot(o_ref[...].astype(BF16), wo_ref[sw:, :])
    h = x_ref[...] + mixed
    h_ref[...] = h
    xnb = _rms(h, fnw_ref[...]).astype(BF16)
    xn_ref[...] = xnb
    logits = _dot(xnb, rw_ref[...]) + rb_ref[...]
    lane = lax.broadcasted_iota(jnp.int32, logits.shape, 1).astype(F32)
    work = logits
    vals, idxs = [], []
    for _ in range(TOP_K):
        mx = jnp.max(work, axis=-1, keepdims=True)
        ix = jnp.min(jnp.where(work == mx, lane, float(LANES)), axis=-1, keepdims=True)
        vals.append(mx)
        idxs.append(ix)
        work = jnp.where(lane == ix, -3e38, work)
    es = [jnp.exp(v - vals[0]) for v in vals]
    den = es[0] + es[1] + es[2] + es[3]
    info = jnp.zeros(logits.shape, F32)
    for k in range(TOP_K):
        info = info + jnp.where(lane == float(k), idxs[k], 0.0) + jnp.where(lane == float(TOP_K + k), es[k] / den, 0.0)
    info_ref[...] = info[:, :2 * TOP_K]
    infot_ref[0] = info.T[:2 * TOP_K, :]


def _mix(ys, u, d_skip, o, x, glu_w_b, glu_b, ssm_norm_w, w_out_b, ffn_norm_w, router_w_b, router_b_p, packed):
    t, d = x.shape
    sw = u.shape[1]
    aw = o.shape[1]
    tm = min(TOKEN_TILE, t)
    nt = t // tm
    row = lambda i: (i, 0)
    fix = lambda i: (0, 0)
    if packed:
        ys_spec = pl.BlockSpec((ys.shape[0], tm // S5_CHUNK, ys.shape[2]), lambda i: (0, i, 0))
        scratch = [pltpu.VMEM((sw // LANES, tm, LANES), F32)]
    else:
        ys_spec = pl.BlockSpec((tm, sw), row)
        scratch = []
    return pl.pallas_call(
        functools.partial(_mix_body, sw=sw, packed=packed),
        grid=(nt,),
        in_specs=[ys_spec, pl.BlockSpec((tm, sw), row), pl.BlockSpec((1, sw), fix),
                  pl.BlockSpec((tm, aw), row), pl.BlockSpec((tm, d), row),
                  pl.BlockSpec((sw, sw), fix), pl.BlockSpec((1, sw), fix), pl.BlockSpec((1, sw), fix),
                  pl.BlockSpec((d, d), fix), pl.BlockSpec((1, d), fix),
                  pl.BlockSpec((d, LANES), fix), pl.BlockSpec((1, LANES), fix)],
        out_specs=[pl.BlockSpec((tm, d), row), pl.BlockSpec((tm, d), row),
                   pl.BlockSpec((tm, 2 * TOP_K), row), pl.BlockSpec((1, 2 * TOP_K, tm), lambda i: (i, 0, 0))],
        out_shape=[jax.ShapeDtypeStruct((t, d), F32), jax.ShapeDtypeStruct((t, d), BF16),
                   jax.ShapeDtypeStruct((t, 2 * TOP_K), F32), jax.ShapeDtypeStruct((nt, 2 * TOP_K, tm), F32)],
        scratch_shapes=scratch,
        compiler_params=_cparams(("parallel",)),
    )(ys, u, d_skip.reshape(1, sw), o, x, glu_w_b, glu_b.reshape(1, sw), ssm_norm_w.reshape(1, sw), w_out_b,
      ffn_norm_w.reshape(1, d), router_w_b, router_b_p)


def _wperm_body(w_ref, p_ref, o_ref):
    f = o_ref.shape[-1] // 2
    grp = 2 * LANES
    for k in range(w_ref.shape[-1] // grp):
        r = _dot(w_ref[0, :, k * grp:(k + 1) * grp].astype(BF16), p_ref[...]).astype(BF16)
        o_ref[0, :, k * LANES:(k + 1) * LANES] = r[:, :LANES]
        o_ref[0, :, f + k * LANES:f + (k + 1) * LANES] = r[:, LANES:]


def _wperm(w_up):
    ne, d, f2 = w_up.shape
    grp = 2 * LANES
    src = jnp.arange(grp)[:, None]
    dst = jnp.arange(grp)[None, :]
    perm = (dst == (src % 2) * LANES + src // 2).astype(BF16)
    return pl.pallas_call(
        _wperm_body,
        grid=(ne,),
        in_specs=[pl.BlockSpec((1, d, f2), lambda e: (e, 0, 0)), pl.BlockSpec((grp, grp), lambda e: (0, 0))],
        out_specs=pl.BlockSpec((1, d, f2), lambda e: (e, 0, 0)),
        out_shape=jax.ShapeDtypeStruct((ne, d, f2), BF16),
        compiler_params=_cparams(("parallel",)),
    )(w_up, perm)


def _seg_rows(tt, ne):
    worst = tt * TOP_K + ne * (UNIT - 1)
    return -(-worst // LANES) * LANES


def _moe_sort_body(x_ref, infot_ref, xs_ref, cp_ref, pos_ref, *, tt, ne, rt):
    infot = infot_ref[0]
    sub = lax.broadcasted_iota(jnp.int32, (ne, tt), 0).astype(F32)
    onehots = [(sub == infot[k:k + 1, :]).astype(F32) for k in range(TOP_K)]
    et = onehots[0] + onehots[1] + onehots[2] + onehots[3]
    cnt = jnp.sum(et, axis=1, keepdims=True)
    cp = jnp.floor((cnt + (UNIT - 1.0)) * (1.0 / UNIT))
    er = lax.broadcasted_iota(jnp.int32, (ne, ne), 0)
    ec = lax.broadcasted_iota(jnp.int32, (ne, ne), 1)
    cpb = jnp.broadcast_to(cp, (ne, LANES))
    lo = _dot((ec < er).astype(BF16), cpb.astype(BF16))[:, :1] * float(UNIT)
    tr = lax.broadcasted_iota(jnp.int32, (tt, tt), 0)
    tc = lax.broadcasted_iota(jnp.int32, (tt, tt), 1)
    rank = _dot(et.astype(BF16), (tr < tc).astype(BF16))
    base = lo + rank
    poss = [jnp.sum(oh * base, axis=0, keepdims=True) for oh in onehots]
    rows = lax.broadcasted_iota(jnp.int32, (rt, tt), 0).astype(F32)
    sel = (rows == poss[0]).astype(F32)
    for k in range(1, TOP_K):
        sel = sel + (rows == poss[k]).astype(F32)
    xs_ref[...] = _dot(sel.astype(BF16), x_ref[...])
    cp_ref[0] = cpb.astype(jnp.int32)
    pt = jnp.concatenate(poss + [jnp.zeros((LANES - TOP_K, tt), F32)], axis=0)
    pos_ref[...] = pt.T[:, :2 * TOP_K]


def _moe_sort(xn, infot, ne):
    t, d = xn.shape
    nt, _, tt = infot.shape
    rt = _seg_rows(tt, ne)
    return pl.pallas_call(
        functools.partial(_moe_sort_body, tt=tt, ne=ne, rt=rt),
        grid=(nt,),
        in_specs=[pl.BlockSpec((tt, d), lambda i: (i, 0)), pl.BlockSpec((1, 2 * TOP_K, tt), lambda i: (i, 0, 0))],
        out_specs=[pl.BlockSpec((rt, d), lambda i: (i, 0)), pl.BlockSpec((1, ne, LANES), lambda i: (i, 0, 0)),
                   pl.BlockSpec((tt, 2 * TOP_K), lambda i: (i, 0))],
        out_shape=[jax.ShapeDtypeStruct((nt * rt, d), F32), jax.ShapeDtypeStruct((nt, ne, LANES), jnp.int32),
                   jax.ShapeDtypeStruct((t, 2 * TOP_K), F32)],
        compiler_params=_cparams(("parallel",)),
    )(xn, infot)


def _moe_tables(cp, rt, bu):
    nt, ne = cp.shape
    ru = rt // UNIT
    lo = jnp.cumsum(cp, axis=1) - cp
    used = jnp.sum(cp, axis=1)
    seg_len = jnp.concatenate([cp.T, (ru - used)[None, :]], axis=0)
    seg_src = jnp.concatenate([(jnp.arange(nt)[:, None] * ru + lo).T,
                               (jnp.arange(nt) * ru + used)[None, :]], axis=0)
    seg_off = jnp.cumsum(seg_len, axis=1) - seg_len
    tot = jnp.sum(seg_len, axis=1)
    eblk = -(-tot // bu)
    bend = jnp.cumsum(eblk)
    bstart = bend - eblk
    nblk = (nt * ru) // bu + ne + 1
    blk = jnp.arange(nblk, dtype=jnp.int32)
    e = jnp.sum((bend[None, :] <= blk[:, None]).astype(jnp.int32), axis=1)
    ec = jnp.minimum(e, ne)
    w = (blk - bstart[ec])[:, None] * bu + jnp.arange(bu, dtype=jnp.int32)[None, :]
    is_pad = (e > ne)[:, None] | (w >= tot[ec][:, None])
    so = seg_off[ec]
    sv = (seg_src - seg_off)[ec]
    inside = so[:, None, :] <= w[:, :, None]
    last = inside & ~jnp.concatenate([inside[:, :, 1:], jnp.zeros_like(inside[:, :, :1])], axis=2)
    unit = (jnp.sum(jnp.where(last, sv[:, None, :], 0), axis=2) + w).astype(jnp.int32)
    dump = nt * ru + jnp.arange(bu, dtype=jnp.int32)[None, :]
    tab = jnp.concatenate([jnp.where(is_pad, 0, unit), jnp.where(is_pad, dump, unit)], axis=1)
    pad_l = -(-2 * bu // LANES) * LANES - 2 * bu
    tab = jnp.pad(tab, ((0, 0), (0, pad_l))).reshape(nblk, 1, -1).astype(jnp.int32)
    nact = bend[ne].astype(jnp.int32).reshape(1)
    return tab, e.astype(jnp.int32), nact, nblk


def _moe_ffn_body(bexp_ref, nact_ref, tab_ref, tabn_ref, xs_hbm, wup_ref, bup_ref, wdn_ref, bdn_ref,
                  out_hbm, xbuf, obuf, sem_in, sem_out, *, bu, ne):
    b = pl.program_id(0)
    nact = nact_ref[0]
    slot = b % 2
    d = xbuf.shape[-1]

    def start_in(tref, s):
        for j in range(bu):
            pltpu.make_async_copy(xs_hbm.at[tref[0, 0, j]], xbuf.at[s, j], sem_in.at[s]).start()

    def wait_in(s):
        for j in range(bu):
            pltpu.make_async_copy(xs_hbm.at[0], xbuf.at[s, j], sem_in.at[s]).wait()

    def start_out():
        for j in range(bu):
            pltpu.make_async_copy(obuf.at[j], out_hbm.at[tab_ref[0, 0, bu + j]], sem_out).start()

    def wait_out():
        for j in range(bu):
            pltpu.make_async_copy(obuf.at[j], out_hbm.at[0], sem_out).wait()

    @pl.when(b < nact)
    def _():
        @pl.when(b == 0)
        def _():
            start_in(tab_ref, 0)

        wait_in(slot)

        @pl.when(b + 1 < nact)
        def _():
            start_in(tabn_ref, 1 - slot)

        e = bexp_ref[b]

        @pl.when(e < ne)
        def _():
            x = xbuf[slot].reshape(bu * UNIT, d).astype(BF16)
            h = _dot(x, wup_ref[0]) + bup_ref[0]
            f = h.shape[1] // 2
            glu = jnp.minimum(h[:, :f], SWIGLU_LIMIT)
            lin = jnp.clip(h[:, f:], -SWIGLU_LIMIT, SWIGLU_LIMIT)
            act = glu * (1.0 / (1.0 + jnp.exp(-SWIGLU_ALPHA * glu))) * (lin + 1.0)
            out = _dot(act.astype(BF16), wdn_ref[0]) + bdn_ref[0]

            @pl.when(b > 0)
            def _():
                wait_out()

            obuf[...] = out.reshape(bu, UNIT, d)

        @pl.when(e >= ne)
        def _():
            @pl.when(b > 0)
            def _():
                wait_out()

            obuf[...] = jnp.zeros(obuf.shape, F32)

        start_out()

        @pl.when(b == nact - 1)
        def _():
            wait_out()
            nu = out_hbm.shape[0] - bu
            for j in range(bu):
                pltpu.make_async_copy(obuf.at[j], out_hbm.at[nu + j], sem_out).start()
            wait_out()


def _moe_ffn(xs, cp, wup_b, bup, wdn_b, bdn, rt):
    ne, d, f2 = wup_b.shape
    bu = FFN_ROWS // UNIT
    tab, bexp, nact, nblk = _moe_tables(cp, rt, bu)
    nu = xs.shape[0] // UNIT
    xs3 = xs.reshape(nu, UNIT, d)
    tl = tab.shape[-1]
    wsel = lambda b, be, na: (jnp.minimum(be[b], ne - 1), 0, 0)
    out = pl.pallas_call(
        functools.partial(_moe_ffn_body, bu=bu, ne=ne),
        grid_spec=pltpu.PrefetchScalarGridSpec(
            num_scalar_prefetch=2,
            grid=(nblk,),
            in_specs=[pl.BlockSpec((1, 1, tl), lambda b, be, na: (b, 0, 0), memory_space=pltpu.SMEM),
                      pl.BlockSpec((1, 1, tl), lambda b, be, na: (jnp.minimum(b + 1, nblk - 1), 0, 0),
                                   memory_space=pltpu.SMEM),
                      pl.BlockSpec(memory_space=pl.ANY),
                      pl.BlockSpec((1, d, f2), wsel), pl.BlockSpec((1, 1, f2), wsel),
                      pl.BlockSpec((1, f2 // 2, d), wsel), pl.BlockSpec((1, 1, d), wsel)],
            out_specs=pl.BlockSpec(memory_space=pl.ANY),
            scratch_shapes=[pltpu.VMEM((2, bu, UNIT, d), F32), pltpu.VMEM((bu, UNIT, d), F32),
                            pltpu.SemaphoreType.DMA((2,)), pltpu.SemaphoreType.DMA(())]),
        out_shape=jax.ShapeDtypeStruct((nu + bu, UNIT, d), F32),
        compiler_params=_cparams(("arbitrary",)),
    )(bexp, nact, tab, tab, xs3, wup_b, bup.reshape(ne, 1, f2), wdn_b, bdn.reshape(ne, 1, d))
    return out.reshape((nu + bu) * UNIT, d)


def _moe_combine_body(o_ref, pos_ref, info_ref, h_ref, fw_ref, y_ref, *, rt):
    pos = pos_ref[...]
    info = info_ref[...]
    lane = lax.broadcasted_iota(jnp.int32, (pos.shape[0], rt), 1).astype(F32)
    selt = jnp.where(lane == pos[:, 0:1], info[:, TOP_K:TOP_K + 1], 0.0)
    for k in range(1, TOP_K):
        selt = selt + jnp.where(lane == pos[:, k:k + 1], info[:, TOP_K + k:TOP_K + k + 1], 0.0)
    y = _dot(selt.astype(BF16), o_ref[...].astype(BF16))
    y_ref[...] = _rms(h_ref[...] + y, fw_ref[...])


def _moe_combine(outs, pos, info, h, final_w, rt):
    t, d = h.shape
    tt = min(TOKEN_TILE, t)
    row = lambda i: (i, 0)
    return pl.pallas_call(
        functools.partial(_moe_combine_body, rt=rt),
        grid=(t // tt,),
        in_specs=[pl.BlockSpec((rt, d), row), pl.BlockSpec((tt, 2 * TOP_K), row), pl.BlockSpec((tt, 2 * TOP_K), row),
                  pl.BlockSpec((tt, d), row), pl.BlockSpec((1, d), lambda i: (0, 0))],
        out_specs=pl.BlockSpec((tt, d), row),
        out_shape=jax.ShapeDtypeStruct((t, d), F32),
        compiler_params=_cparams(("parallel",)),
    )(outs, pos, info, h, final_w.reshape(1, d))


def _group(x3, h0_re, h0_im, past, tc, wts):
    n, l, d = x3.shape
    x = x3.reshape(n * l, d)
    sw, aw = wts['sw'], wts['aw']
    hd = aw // (2 * ATT_HEADS)
    prompt = past is None
    proj = _inproj(x, wts['attn_norm_w'], wts['w_in'], sw, aw, hd ** -0.5, pack=prompt)
    u, k, v, qb, kb = proj[:5]
    if prompt:
        vt, u2 = proj[5:]
        ys, st_re, st_im = _s5(u2, h0_re, h0_im, wts['s5'][tc], n, l, tc, packed=True)
        o = _attn_prompt(qb, kb, vt, wts['lam'], wts['subln_w'], n, l, wts['post'])
    else:
        ys, st_re, st_im = _s5(u, h0_re, h0_im, wts['s5'][tc], n, l, tc, packed=False)
        o = _attn_sample(qb, kb, proj[5], past[0], past[1], past[2], wts['lam'], wts['subln_w'], n, l, wts['post'])
    h, xn, info, infot = _mix(ys, u, wts['d_skip'], o, x, wts['glu_w'], wts['glu_b'], wts['ssm_norm_w'],
                              wts['w_out'], wts['ffn_norm_w'], wts['router_w'], wts['router_b'], packed=prompt)
    ne = wts['w_up'].shape[0]
    rt = _seg_rows(infot.shape[2], ne)
    xs, cpl, pos = _moe_sort(xn, infot, ne)
    outs = _moe_ffn(xs, cpl[:, :, 0], wts['w_up'], wts['b_up'], wts['w_down'], wts['b_down'], rt)
    y = _moe_combine(outs, pos, info, h, wts['final_norm_w'], rt)
    heads = (n, l, ATT_HEADS)
    return (y.reshape(n, l, d), k.reshape(heads + (2, hd))[None], v.reshape(heads + (2 * hd,))[None],
            st_re[None], st_im[None])


def kernel(x_prompt, x_sample, cache_k, cache_v, state_ssm_re, state_ssm_im, page_table, attn_norm_w, w_in, ssm_lambda_re, ssm_lambda_im, ssm_log_dt, ssm_b_re, ssm_b_im, ssm_c_re, ssm_c_im, ssm_d, ssm_glu_w, ssm_glu_b, ssm_norm_w, diff_lambda_q1, diff_lambda_k1, diff_lambda_q2, diff_lambda_k2, subln_w, w_out, ffn_norm_w, router_w, router_b, w_up, b_up, w_down, b_down, final_norm_w):
    assert w_in.shape[0] == 1, "single-layer trunk"
    g = ssm_lambda_re.shape[1]
    sw = g * SSM_GROUP
    aw = (w_in.shape[-1] - sw) // 3
    ne = router_w.shape[-1]
    lam_init = 0.8 - 0.6 * math.exp(-0.3 * 0)
    lam = (jnp.exp(jnp.sum(diff_lambda_q1[0] * diff_lambda_k1[0]))
           - jnp.exp(jnp.sum(diff_lambda_q2[0] * diff_lambda_k2[0])) + lam_init).astype(F32)
    t_dec = x_sample.shape[1]
    s5_args = (ssm_lambda_re[0], ssm_lambda_im[0], ssm_log_dt[0], ssm_b_re[0], ssm_b_im[0],
               ssm_c_re[0], ssm_c_im[0])
    wts = {
        'sw': sw, 'aw': aw, 'lam': lam, 'post': 1.0 - lam_init,
        'attn_norm_w': attn_norm_w[0], 'w_in': w_in[0].astype(BF16),
        's5': {tc: _s5_params(*s5_args, tc) for tc in {S5_CHUNK, t_dec}}, 'd_skip': ssm_d[0],
        'glu_w': ssm_glu_w[0].astype(BF16), 'glu_b': ssm_glu_b[0], 'ssm_norm_w': ssm_norm_w[0],
        'subln_w': subln_w[0], 'w_out': w_out[0].astype(BF16), 'ffn_norm_w': ffn_norm_w[0],
        'router_w': jnp.pad(router_w[0], ((0, 0), (0, LANES - ne))).astype(BF16),
        'router_b': jnp.pad(router_b[0], (0, LANES - ne), constant_values=NEG_BIG).reshape(1, LANES),
        'w_up': _wperm(w_up[0]),
        'b_up': jnp.concatenate([b_up[0, :, 0::2], b_up[0, :, 1::2]], axis=-1),
        'w_down': w_down[0].astype(BF16), 'b_down': b_down[0], 'final_norm_w': final_norm_w,
    }
    nb = x_prompt.shape[0]
    zeros = jnp.zeros((nb, g, SSM_STATE), F32)
    yp, kp, vp, srp, sip = _group(x_prompt, zeros, zeros, None, S5_CHUNK, wts)
    ys, ks, vs, srs, sis = _group(x_sample, state_ssm_re[0], state_ssm_im[0],
                                  (cache_k[0], cache_v[0], page_table), t_dec, wts)
    return (yp, ys, kp, vp, srp, sip, ks, vs, srs, sis)
```

```python
import functools
import math

import jax
import jax.numpy as jnp
from jax import lax
from jax.experimental import pallas as pl
from jax.experimental.pallas import tpu as pltpu

F32 = jnp.float32
BF16 = jnp.bfloat16

SSM_GROUP = 16
SSM_STATE = 64
ATT_HEADS = 4
TOP_K = 4
SWIGLU_LIMIT = 7.0
SWIGLU_ALPHA = 1.702
RMS_EPS = 1e-6
NEG_BIG = -1e30

LANES = 128
SUBLANES = 8
VMEM_LIMIT = 56 * 1024 * 1024

INPROJ_ROWS = 512
ATTN_Q = 256
ATTN_K = 512
S5_CHUNK = 16
S5_ROWS = 512
TOKEN_TILE = 256
FFN_ROWS = 512
UNIT = SUBLANES
DEC_PAGES = 8


def _cparams(sem):
    return pltpu.CompilerParams(dimension_semantics=sem, vmem_limit_bytes=VMEM_LIMIT)


def _rms(x, w):
    ms = jnp.mean(x * x, axis=-1, keepdims=True)
    return x * lax.rsqrt(ms + RMS_EPS) * w


def _dot(a, b):
    return jnp.dot(a, b, preferred_element_type=F32)


def _dot_nt(a, b):
    return lax.dot_general(a, b, (((1,), (1,)), ((), ())), preferred_element_type=F32)


def _pack_pairs(x):
    half = x.shape[1] // 2
    lo = lax.bitcast_convert_type(x[:, :half], jnp.uint32)
    hi = lax.bitcast_convert_type(x[:, half:], jnp.uint32)
    return lax.shift_right_logical(lo, jnp.uint32(16)) | (hi & jnp.uint32(0xFFFF0000))


def _unpack_pairs(w):
    lo = lax.bitcast_convert_type(lax.shift_left(w, jnp.uint32(16)), F32)
    hi = lax.bitcast_convert_type(w & jnp.uint32(0xFFFF0000), F32)
    return jnp.concatenate([lo, hi], axis=1)


def _lane_mask(shape, lo, width):
    lane = lax.broadcasted_iota(jnp.int32, shape, 1)
    return (lane >= lo) & (lane < lo + width)


def _inproj_body(x_ref, nw_ref, w_ref, *refs, sw, aw, scale, pack):
    xn = _rms(x_ref[...], nw_ref[...])
    proj = _dot(xn.astype(BF16), w_ref[...])
    q = proj[:, sw:sw + aw]
    k = proj[:, sw + aw:sw + 2 * aw]
    v = proj[:, sw + 2 * aw:]
    if not pack:
        u_ref, k_ref, v_ref, qb_ref, kb_ref, vb_ref = refs
        vb_ref[...] = v.astype(BF16)
    else:
        u_ref, k_ref, v_ref, qb_ref, kb_ref, vt_ref, u2_ref, u_s = refs
        for vt in range(sw // LANES):
            u_s[vt] = proj[:, vt * LANES:(vt + 1) * LANES]
        vt_ref[0] = v.T.astype(BF16)
    u_ref[...] = proj[:, :sw]
    k_ref[...] = k
    v_ref[...] = v
    qb_ref[...] = (q * scale).astype(BF16)
    kb_ref[...] = k.astype(BF16)
    if pack:
        nk = u_s.shape[1] // S5_CHUNK
        per = LANES // SSM_GROUP
        for vt in range(sw // LANES):
            for half in range(S5_CHUNK // per):
                accs = [None] * per
                for s8 in range(per):
                    xv = u_s[vt, pl.ds(half * per + s8, nk, stride=S5_CHUNK), :]
                    mask = _lane_mask(xv.shape, s8 * SSM_GROUP, SSM_GROUP)
                    for gp in range(per):
                        shift = (SSM_GROUP * (s8 - gp)) % LANES
                        r = pltpu.roll(xv, shift, axis=1) if shift else xv
                        accs[gp] = r if s8 == 0 else jnp.where(mask, r, accs[gp])
                for gp in range(per):
                    u2_ref[vt * per + gp, :, half * LANES:(half + 1) * LANES] = accs[gp].astype(BF16)


def _inproj(x, norm_w, w_in_b, sw, aw, scale, pack):
    t, d = x.shape
    tm = min(INPROJ_ROWS, t)
    row = lambda i: (i, 0)
    fix = lambda i: (0, 0)
    f32o = lambda w: jax.ShapeDtypeStruct((t, w), F32)
    b16o = lambda w: jax.ShapeDtypeStruct((t, w), BF16)
    out_specs = [pl.BlockSpec((tm, sw), row)] + [pl.BlockSpec((tm, aw), row)] * 4
    out_shape = [f32o(sw), f32o(aw), f32o(aw), b16o(aw), b16o(aw)]
    scratch = []
    if pack:
        g = sw // SSM_GROUP
        kd = S5_CHUNK * SSM_GROUP
        out_specs += [pl.BlockSpec((1, aw, tm), lambda i: (i, 0, 0)),
                      pl.BlockSpec((g, tm // S5_CHUNK, kd), lambda i: (0, i, 0))]
        out_shape += [jax.ShapeDtypeStruct((t // tm, aw, tm), BF16),
                      jax.ShapeDtypeStruct((g, t // S5_CHUNK, kd), BF16)]
        scratch = [pltpu.VMEM((sw // LANES, tm, LANES), F32)]
    else:
        out_specs += [pl.BlockSpec((tm, aw), row)]
        out_shape += [b16o(aw)]
    return pl.pallas_call(
        functools.partial(_inproj_body, sw=sw, aw=aw, scale=scale, pack=pack),
        grid=(t // tm,),
        in_specs=[pl.BlockSpec((tm, d), row), pl.BlockSpec((1, d), fix), pl.BlockSpec(w_in_b.shape, fix)],
        out_specs=out_specs, out_shape=out_shape, scratch_shapes=scratch,
        compiler_params=_cparams(("parallel",)),
    )(x, norm_w.reshape(1, d), w_in_b)


def _s5_params(lam_re, lam_im, log_dt, b_re, b_im, c_re, c_im, tc):
    hp = lax.Precision.HIGHEST
    g, p = lam_re.shape
    c = b_re.shape[-1]
    dt = jnp.exp(log_dt)[:, None]
    mag = jnp.exp(lam_re * dt)
    a_re = mag * jnp.cos(lam_im * dt)
    a_im = mag * jnp.sin(lam_im * dt)
    den = lam_re * lam_re + lam_im * lam_im
    nr = a_re - 1.0
    cf_re = (nr * lam_re + a_im * lam_im) / den
    cf_im = (a_im * lam_re - nr * lam_im) / den
    bb_re = cf_re[..., None] * b_re - cf_im[..., None] * b_im
    bb_im = cf_re[..., None] * b_im + cf_im[..., None] * b_re
    pr, pi = [jnp.ones_like(a_re)], [jnp.zeros_like(a_re)]
    for _ in range(tc):
        pr, pi = pr + [pr[-1] * a_re - pi[-1] * a_im], pi + [pr[-1] * a_im + pi[-1] * a_re]
    pr = jnp.stack(pr)
    pi = jnp.stack(pi)
    ab_re = pr[:tc, :, :, None] * bb_re - pi[:tc, :, :, None] * bb_im
    ab_im = pr[:tc, :, :, None] * bb_im + pi[:tc, :, :, None] * bb_re
    kern = (jnp.einsum('gdp,tgpc->tgdc', c_re, ab_re, precision=hp)
            - jnp.einsum('gdp,tgpc->tgdc', c_im, ab_im, precision=hp))
    lag = jnp.arange(tc)[None, :] - jnp.arange(tc)[:, None]
    kt = kern[jnp.clip(lag, 0, tc - 1)]
    kt = jnp.where((lag >= 0)[:, :, None, None, None], kt, 0.0)
    m = kt.transpose(2, 0, 4, 1, 3).reshape(g, tc * c, tc * c)
    wr = ab_re[::-1].transpose(1, 0, 3, 2).reshape(g, tc * c, p)
    wi = ab_im[::-1].transpose(1, 0, 3, 2).reshape(g, tc * c, p)
    w = jnp.concatenate([wr, wi], axis=-1)
    wsw = jnp.concatenate([wi, wr], axis=-1)
    e_re = c_re[None] * pr[1:, :, None, :] - c_im[None] * pi[1:, :, None, :]
    e_im = c_re[None] * pi[1:, :, None, :] + c_im[None] * pr[1:, :, None, :]
    v = jnp.concatenate([e_re.transpose(1, 3, 0, 2).reshape(g, p, tc * c),
                         -e_im.transpose(1, 3, 0, 2).reshape(g, p, tc * c)], axis=1)
    ar = jnp.concatenate([pr[tc], pr[tc]], axis=-1)[:, None, :]
    ai = jnp.concatenate([-pi[tc], pi[tc]], axis=-1)[:, None, :]
    return m.astype(BF16), w.astype(BF16), wsw.astype(BF16), v.astype(BF16), ar, ai


def _s5_body(u_ref, m_ref, w_ref, wsw_ref, v_ref, ar_ref, ai_ref, h0_ref, h0s_ref,
             y_ref, hout_ref, h_s, hs_s, s_s, ssw_s, hin_s, *, cb, nb, seq_major):
    @pl.when(pl.program_id(1) == 0)
    def _():
        h_s[...] = h0_ref[0]
        hs_s[...] = h0s_ref[0]

    kd = u_ref.shape[-1]
    ub = u_ref[0].reshape(cb * nb, kd).astype(BF16)
    yi = _dot(ub, m_ref[0])
    s_s[...] = _dot(ub, w_ref[0])
    ssw_s[...] = _dot(ub, wsw_ref[0])
    ar = ar_ref[0]
    ai = ai_ref[0]
    h = h_s[...]
    hs = hs_s[...]
    for c in range(cb):
        idx = pl.ds(c, nb, stride=cb) if (seq_major and cb > 1) else pl.ds(c * nb, nb)
        hin_s[idx, :] = h
        h, hs = ar * h + ai * hs + s_s[idx, :], ar * hs - ai * h + ssw_s[idx, :]
    h_s[...] = h
    hs_s[...] = hs
    hout_ref[0] = h
    yo = _dot(hin_s[...].astype(BF16), v_ref[0])
    y_ref[0] = (yi + yo).reshape(y_ref.shape[1:])


def _s5(u_in, h0_re, h0_im, prm, n, l, tc, packed):
    m, w, wsw, v, ar, ai = prm
    g = m.shape[0]
    c = SSM_GROUP
    p2 = 2 * SSM_STATE
    nc = l // tc
    kd = tc * c
    cb = max(1, min(nc, S5_ROWS // n))
    assert nc % cb == 0 and n % SUBLANES == 0
    if packed:
        u2 = u_in.reshape(g, n, nc, kd)
        blk_shape, blk = (1, n, cb, kd), (lambda gi, j: (gi, 0, j, 0))
    else:
        u2 = u_in.reshape(n, nc, tc, g, c).transpose(3, 1, 0, 2, 4).reshape(g, nc, n, kd).astype(BF16)
        blk_shape, blk = (1, cb, n, kd), (lambda gi, j: (gi, j, 0, 0))
    h0 = jnp.concatenate([h0_re, h0_im], axis=-1).transpose(1, 0, 2)
    h0s = jnp.concatenate([h0_im, h0_re], axis=-1).transpose(1, 0, 2)
    par = lambda gi, j: (gi, 0, 0)
    y2, hout = pl.pallas_call(
        functools.partial(_s5_body, cb=cb, nb=n, seq_major=packed),
        grid=(g, nc // cb),
        in_specs=[pl.BlockSpec(blk_shape, blk),
                  pl.BlockSpec((1, kd, kd), par), pl.BlockSpec((1, kd, p2), par), pl.BlockSpec((1, kd, p2), par),
                  pl.BlockSpec((1, p2, kd), par),
                  pl.BlockSpec((1, 1, p2), par), pl.BlockSpec((1, 1, p2), par),
                  pl.BlockSpec((1, n, p2), par), pl.BlockSpec((1, n, p2), par)],
        out_specs=[pl.BlockSpec(blk_shape, blk), pl.BlockSpec((1, n, p2), par)],
        out_shape=[jax.ShapeDtypeStruct(u2.shape, F32), jax.ShapeDtypeStruct((g, n, p2), F32)],
        scratch_shapes=[pltpu.VMEM((n, p2), F32), pltpu.VMEM((n, p2), F32), pltpu.VMEM((cb * n, p2), F32),
                        pltpu.VMEM((cb * n, p2), F32), pltpu.VMEM((cb * n, p2), F32)],
        compiler_params=_cparams(("parallel", "arbitrary")),
    )(u2, m, w, wsw, v, ar, ai, h0, h0s)
    if packed:
        y = y2.reshape(g, n * nc, kd)
    else:
        y = y2.reshape(g, nc, n, tc, c).transpose(2, 1, 3, 0, 4).reshape(n * l, g * c)
    hout = hout.transpose(1, 0, 2)
    return y, hout[..., :SSM_STATE], hout[..., SSM_STATE:]


def _attn_body(lam_ref, q_ref, k_ref, vt_ref, swb_ref, o_ref, q2_s, m_s, acc_s, *, tq, tk, vd, post):
    qi = pl.program_id(2)
    hd = vd // 2
    q = q_ref[0]
    lane = lax.broadcasted_iota(jnp.int32, q.shape, 1)
    zero = jnp.zeros_like(q)
    q2_s[0:tq, :] = jnp.where(lane < hd, q, zero)
    q2_s[tq:, :] = jnp.where(lane >= hd, q, zero)
    m_s[...] = jnp.full(m_s.shape, NEG_BIG, F32)
    acc_s[...] = jnp.zeros(acc_s.shape, F32)
    ones = jnp.ones((vd, tk), BF16)

    def scores(j):
        start = pl.multiple_of(j * tk, tk)
        return _dot_nt(k_ref[0, pl.ds(start, tk), :], q2_s[...])

    def step(j, st2, off):
        vaug = jnp.concatenate([vt_ref[j], ones], axis=0)
        for c in range(2):
            cs = slice(c * tq, (c + 1) * tq)
            st = st2[:, cs]
            if off is not None:
                key = lax.broadcasted_iota(jnp.int32, st.shape, 0)
                row = lax.broadcasted_iota(jnp.int32, st.shape, 1)
                st = jnp.where(key <= row + off, st, NEG_BIG)
            m_old = m_s[:, cs]
            m_new = jnp.maximum(m_old, jnp.max(st, axis=0, keepdims=True))
            alpha = jnp.exp(m_old - m_new)
            p = jnp.exp(st - m_new).astype(BF16)
            acc_s[:, cs] = alpha * acc_s[:, cs] + _dot(vaug, p)
            m_s[:, cs] = m_new

    def full_step(j, st2):
        nxt = scores(j + 1)
        step(j, st2, None)
        return nxt

    nfull = (qi * tq) // tk
    st2 = lax.fori_loop(0, nfull, full_step, scores(0))
    step(nfull, st2, qi * tq - nfull * tk)
    acc = acc_s[...]
    ot = (acc[:vd, :tq] / acc[vd:vd + 1, :tq]) - lam_ref[0] * (acc[:vd, tq:] / acc[vd:vd + 1, tq:])
    ms = jnp.mean(ot * ot, axis=0, keepdims=True)
    ot = ot * lax.rsqrt(ms + RMS_EPS) * swb_ref[...] * post
    o_ref[0] = ot.T.astype(o_ref.dtype)


def _attn_prompt(qb, kb, vt, lam, subln_w, n, l, post):
    aw = qb.shape[-1]
    vd = aw // ATT_HEADS
    tk = vt.shape[-1]
    tq = min(ATTN_Q, l)
    assert l % tk == 0 and tk % tq == 0
    q3, k3 = qb.reshape(n, l, aw), kb.reshape(n, l, aw)
    swb = jnp.broadcast_to(subln_w.reshape(vd, 1), (vd, tq))
    return pl.pallas_call(
        functools.partial(_attn_body, tq=tq, tk=tk, vd=vd, post=post),
        grid=(n, ATT_HEADS, l // tq),
        in_specs=[pl.BlockSpec(memory_space=pltpu.SMEM),
                  pl.BlockSpec((1, tq, vd), lambda b, h, i: (b, i, h)),
                  pl.BlockSpec((1, l, vd), lambda b, h, i: (b, 0, h)),
                  pl.BlockSpec((l // tk, vd, tk), lambda b, h, i: (b, h, 0)),
                  pl.BlockSpec((vd, tq), lambda b, h, i: (0, 0))],
        out_specs=pl.BlockSpec((1, tq, vd), lambda b, h, i: (b, i, h)),
        out_shape=jax.ShapeDtypeStruct((n, l, aw), BF16),
        scratch_shapes=[pltpu.VMEM((2 * tq, vd), BF16), pltpu.VMEM((1, 2 * tq), F32),
                        pltpu.VMEM((2 * vd, 2 * tq), F32)],
        compiler_params=_cparams(("parallel", "parallel", "arbitrary")),
    )(lam.reshape(1), q3, k3, vt, swb).reshape(n * l, aw)


def _attn_dec_body(pt_ref, lam_ref, qr_ref, kn_ref, vn_ref, sw_ref, *rest, pg, t_new, vd, post):
    k_refs, v_refs = rest[:pg], rest[pg:2 * pg]
    o_ref, m_s, l_s, acc_s = rest[2 * pg:]
    j = pl.program_id(1)

    @pl.when(j == 0)
    def _():
        m_s[...] = jnp.full(m_s.shape, NEG_BIG, F32)
        l_s[...] = jnp.zeros(l_s.shape, F32)
        acc_s[...] = jnp.zeros(acc_s.shape, F32)

    qr = qr_ref[0]
    rows = 2 * t_new
    page = k_refs[0].shape[2]

    def update(s, vhs):
        m_old = m_s[...]
        m_new = jnp.maximum(m_old, jnp.max(s, axis=-1, keepdims=True))
        alpha = jnp.exp(m_old - m_new)
        p = jnp.exp(s - m_new).astype(BF16)
        l_s[...] = alpha * l_s[...] + jnp.sum(p.astype(F32), axis=-1, keepdims=True)
        for h in range(ATT_HEADS):
            hs = slice(h * rows, (h + 1) * rows)
            pv = _dot(p[hs, :page], vhs[h][0])
            for i in range(1, len(vhs[h])):
                pv = pv + _dot(p[hs, i * page:(i + 1) * page], vhs[h][i])
            acc_s[hs, :] = alpha[hs] * acc_s[hs, :] + pv
        m_s[...] = m_new

    s = jnp.concatenate([_dot(qr, kr[0].astype(BF16)) for kr in k_refs], axis=1)
    update(s, [[vr[0, pl.ds(h, page, stride=ATT_HEADS), :].astype(BF16) for vr in v_refs]
               for h in range(ATT_HEADS)])

    @pl.when(j == pl.num_programs(1) - 1)
    def _():
        sn = _dot_nt(qr, kn_ref[0])
        r = lax.broadcasted_iota(jnp.int32, sn.shape, 0) % t_new
        c = lax.broadcasted_iota(jnp.int32, sn.shape, 1)
        vn = vn_ref[0]
        update(jnp.where(c <= r, sn, NEG_BIG), [[vn[:, h * vd:(h + 1) * vd]] for h in range(ATT_HEADS)])
        acc = acc_s[...]
        l = l_s[...]
        for h in range(ATT_HEADS):
            blk = acc[h * rows:(h + 1) * rows] / l[h * rows:(h + 1) * rows]
            o = blk[:t_new] - lam_ref[0] * blk[t_new:]
            o_ref[0, :, h * vd:(h + 1) * vd] = _rms(o, sw_ref[...]) * post


def _attn_sample(qb, kb, vb, cache_k, cache_v, page_table, lam, subln_w, nseq, t_new, post):
    aw = qb.shape[-1]
    vd = aw // ATT_HEADS
    hd = vd // 2
    n_pool, page = cache_k.shape[0], cache_k.shape[1]
    n_pages = page_table.shape[1]
    pg = math.gcd(DEC_PAGES, n_pages)
    ck = cache_k.reshape(n_pool, page, aw).transpose(0, 2, 1)
    cv = cache_v.reshape(n_pool, page * ATT_HEADS, vd)
    q5 = qb.reshape(nseq, t_new, ATT_HEADS, 2, hd)
    qr = jnp.einsum('bqhcd,hx,cy->bhcqxyd', q5, jnp.eye(ATT_HEADS, dtype=BF16), jnp.eye(2, dtype=BF16))
    nr = ATT_HEADS * 2 * t_new
    qr = qr.reshape(nseq, nr, aw)
    pad = ((0, 0), (0, page - t_new), (0, 0))
    kn = jnp.pad(kb.reshape(nseq, t_new, aw), pad)
    vn = jnp.pad(vb.reshape(nseq, t_new, aw), pad)
    seq = lambda b, j, pt: (b, 0, 0)
    pick = lambda b, j, pt, i: (pt[b, j * pg + i], 0, 0)
    k_specs = [pl.BlockSpec((1, aw, page), functools.partial(pick, i=i)) for i in range(pg)]
    v_specs = [pl.BlockSpec((1, page * ATT_HEADS, vd), functools.partial(pick, i=i)) for i in range(pg)]
    return pl.pallas_call(
        functools.partial(_attn_dec_body, pg=pg, t_new=t_new, vd=vd, post=post),
        grid_spec=pltpu.PrefetchScalarGridSpec(
            num_scalar_prefetch=1,
            grid=(nseq, n_pages // pg),
            in_specs=[pl.BlockSpec(memory_space=pltpu.SMEM),
                      pl.BlockSpec((1, nr, aw), seq), pl.BlockSpec((1, page, aw), seq),
                      pl.BlockSpec((1, page, aw), seq), pl.BlockSpec((1, vd), lambda b, j, pt: (0, 0))]
            + k_specs + v_specs,
            out_specs=pl.BlockSpec((1, t_new, aw), seq),
            scratch_shapes=[pltpu.VMEM((nr, 1), F32), pltpu.VMEM((nr, 1), F32), pltpu.VMEM((nr, vd), F32)]),
        out_shape=jax.ShapeDtypeStruct((nseq, t_new, aw), F32),
        compiler_params=_cparams(("parallel", "arbitrary")),
    )(page_table, lam.reshape(1), qr, kn, vn, subln_w.reshape(1, vd), *([ck] * pg), *([cv] * pg)
      ).reshape(nseq * t_new, aw)


def _mix_body(ys_ref, u_ref, dsk_ref, o_ref, x_ref, gw_ref, gb_ref, snw_ref, wo_ref, fnw_ref, rw_ref, rb_ref,
              h_ref, xn_ref, info_ref, infot_ref, *scratch, sw, packed):
    if packed:
        ys_s, = scratch
        nk = ys_s.shape[1] // S5_CHUNK
        per = LANES // SSM_GROUP
        for vt in range(sw // LANES):
            for half in range(S5_CHUNK // per):
                srcs = [ys_ref[vt * per + gp, :, half * LANES:(half + 1) * LANES] for gp in range(per)]
                masks = [_lane_mask(srcs[0].shape, gp * SSM_GROUP, SSM_GROUP) for gp in range(per)]
                for s8 in range(per):
                    val = None
                    for gp in range(per):
                        shift = (SSM_GROUP * (gp - s8)) % LANES
                        r = pltpu.roll(srcs[gp], shift, axis=1) if shift else srcs[gp]
                        val = r if gp == 0 else jnp.where(masks[gp], r, val)
                    ys_s[vt, pl.ds(half * per + s8, nk, stride=S5_CHUNK), :] = val
        ys = jnp.concatenate([ys_s[vt] for vt in range(sw // LANES)], axis=1)
    else:
        ys = ys_ref[...]
    y = jax.nn.gelu(ys + dsk_ref[...] * u_ref[...])
    z = _dot(y.astype(BF16), gw_ref[...]) + gb_ref[...]
    y = y * (1.0 / (1.0 + jnp.exp(-z)))
    yn = _rms(y, snw_ref[...])
    mixed = _dot(yn.astype(BF16), wo_ref[:sw, :]) + _dot(o_ref[...].astype(BF16), wo_ref[sw:, :])
    h = x_ref[...] + mixed
    h_ref[...] = h
    xnb = _rms(h, fnw_ref[...]).astype(BF16)
    xn_ref[...] = xnb
    logits = _dot(xnb, rw_ref[...]) + rb_ref[...]
    lane = lax.broadcasted_iota(jnp.int32, logits.shape, 1).astype(F32)
    work = logits
    vals, idxs = [], []
    for _ in range(TOP_K):
        mx = jnp.max(work, axis=-1, keepdims=True)
        ix = jnp.min(jnp.where(work == mx, lane, float(LANES)), axis=-1, keepdims=True)
        vals.append(mx)
        idxs.append(ix)
        work = jnp.where(lane == ix, -3e38, work)
    es = [jnp.exp(v - vals[0]) for v in vals]
    den = es[0] + es[1] + es[2] + es[3]
    info = jnp.zeros(logits.shape, F32)
    for k in range(TOP_K):
        info = info + jnp.where(lane == float(k), idxs[k], 0.0) + jnp.where(lane == float(TOP_K + k), es[k] / den, 0.0)
    info_ref[...] = info[:, :2 * TOP_K]
    infot_ref[0] = info.T[:2 * TOP_K, :]


def _mix(ys, u, d_skip, o, x, glu_w_b, glu_b, ssm_norm_w, w_out_b, ffn_norm_w, router_w_b, router_b_p, packed):
    t, d = x.shape
    sw = u.shape[1]
    aw = o.shape[1]
    tm = min(TOKEN_TILE, t)
    nt = t // tm
    row = lambda i: (i, 0)
    fix = lambda i: (0, 0)
    if packed:
        ys_spec = pl.BlockSpec((ys.shape[0], tm // S5_CHUNK, ys.shape[2]), lambda i: (0, i, 0))
        scratch = [pltpu.VMEM((sw // LANES, tm, LANES), F32)]
    else:
        ys_spec = pl.BlockSpec((tm, sw), row)
        scratch = []
    return pl.pallas_call(
        functools.partial(_mix_body, sw=sw, packed=packed),
        grid=(nt,),
        in_specs=[ys_spec, pl.BlockSpec((tm, sw), row), pl.BlockSpec((1, sw), fix),
                  pl.BlockSpec((tm, aw), row), pl.BlockSpec((tm, d), row),
                  pl.BlockSpec((sw, sw), fix), pl.BlockSpec((1, sw), fix), pl.BlockSpec((1, sw), fix),
                  pl.BlockSpec((d, d), fix), pl.BlockSpec((1, d), fix),
                  pl.BlockSpec((d, LANES), fix), pl.BlockSpec((1, LANES), fix)],
        out_specs=[pl.BlockSpec((tm, d), row), pl.BlockSpec((tm, d), row),
                   pl.BlockSpec((tm, 2 * TOP_K), row), pl.BlockSpec((1, 2 * TOP_K, tm), lambda i: (i, 0, 0))],
        out_shape=[jax.ShapeDtypeStruct((t, d), F32), jax.ShapeDtypeStruct((t, d), BF16),
                   jax.ShapeDtypeStruct((t, 2 * TOP_K), F32), jax.ShapeDtypeStruct((nt, 2 * TOP_K, tm), F32)],
        scratch_shapes=scratch,
        compiler_params=_cparams(("parallel",)),
    )(ys, u, d_skip.reshape(1, sw), o, x, glu_w_b, glu_b.reshape(1, sw), ssm_norm_w.reshape(1, sw), w_out_b,
      ffn_norm_w.reshape(1, d), router_w_b, router_b_p)


def _wperm_body(w_ref, p_ref, o_ref):
    f = o_ref.shape[-1] // 2
    grp = 2 * LANES
    for k in range(w_ref.shape[-1] // grp):
        r = _dot(w_ref[0, :, k * grp:(k + 1) * grp].astype(BF16), p_ref[...]).astype(BF16)
        o_ref[0, :, k * LANES:(k + 1) * LANES] = r[:, :LANES]
        o_ref[0, :, f + k * LANES:f + (k + 1) * LANES] = r[:, LANES:]


def _wperm(w_up):
    ne, d, f2 = w_up.shape
    grp = 2 * LANES
    src = jnp.arange(grp)[:, None]
    dst = jnp.arange(grp)[None, :]
    perm = (dst == (src % 2) * LANES + src // 2).astype(BF16)
    return pl.pallas_call(
        _wperm_body,
        grid=(ne,),
        in_specs=[pl.BlockSpec((1, d, f2), lambda e: (e, 0, 0)), pl.BlockSpec((grp, grp), lambda e: (0, 0))],
        out_specs=pl.BlockSpec((1, d, f2), lambda e: (e, 0, 0)),
        out_shape=jax.ShapeDtypeStruct((ne, d, f2), BF16),
        compiler_params=_cparams(("parallel",)),
    )(w_up, perm)


def _seg_rows(tt, ne):
    worst = tt * TOP_K + ne * (UNIT - 1)
    return -(-worst // LANES) * LANES


def _moe_sort_body(x_ref, infot_ref, xs_ref, cp_ref, pos_ref, *, tt, ne, rt):
    infot = infot_ref[0]
    sub = lax.broadcasted_iota(jnp.int32, (ne, tt), 0).astype(F32)
    onehots = [(sub == infot[k:k + 1, :]).astype(F32) for k in range(TOP_K)]
    et = onehots[0] + onehots[1] + onehots[2] + onehots[3]
    cnt = jnp.sum(et, axis=1, keepdims=True)
    cp = jnp.floor((cnt + (UNIT - 1.0)) * (1.0 / UNIT))
    er = lax.broadcasted_iota(jnp.int32, (ne, ne), 0)
    ec = lax.broadcasted_iota(jnp.int32, (ne, ne), 1)
    cpb = jnp.broadcast_to(cp, (ne, LANES))
    lo = _dot((ec < er).astype(BF16), cpb.astype(BF16))[:, :1] * float(UNIT)
    tr = lax.broadcasted_iota(jnp.int32, (tt, tt), 0)
    tc = lax.broadcasted_iota(jnp.int32, (tt, tt), 1)
    rank = _dot(et.astype(BF16), (tr < tc).astype(BF16))
    base = lo + rank
    poss = [jnp.sum(oh * base, axis=0, keepdims=True) for oh in onehots]
    rows = lax.broadcasted_iota(jnp.int32, (rt, tt), 0).astype(F32)
    sel = (rows == poss[0]).astype(F32)
    for k in range(1, TOP_K):
        sel = sel + (rows == poss[k]).astype(F32)
    xs_ref[...] = _pack_pairs(_dot(sel.astype(BF16), x_ref[...]))
    cp_ref[0] = cpb.astype(jnp.int32)
    pt = jnp.concatenate(poss + [jnp.zeros((LANES - TOP_K, tt), F32)], axis=0)
    pos_ref[...] = pt.T[:, :2 * TOP_K]


def _moe_sort(xn, infot, ne):
    t, d = xn.shape
    nt, _, tt = infot.shape
    rt = _seg_rows(tt, ne)
    return pl.pallas_call(
        functools.partial(_moe_sort_body, tt=tt, ne=ne, rt=rt),
        grid=(nt,),
        in_specs=[pl.BlockSpec((tt, d), lambda i: (i, 0)), pl.BlockSpec((1, 2 * TOP_K, tt), lambda i: (i, 0, 0))],
        out_specs=[pl.BlockSpec((rt, d // 2), lambda i: (i, 0)), pl.BlockSpec((1, ne, LANES), lambda i: (i, 0, 0)),
                   pl.BlockSpec((tt, 2 * TOP_K), lambda i: (i, 0))],
        out_shape=[jax.ShapeDtypeStruct((nt * rt, d // 2), jnp.uint32),
                   jax.ShapeDtypeStruct((nt, ne, LANES), jnp.int32),
                   jax.ShapeDtypeStruct((t, 2 * TOP_K), F32)],
        compiler_params=_cparams(("parallel",)),
    )(xn, infot)


def _moe_tables(cp, rt, bu):
    nt, ne = cp.shape
    ru = rt // UNIT
    lo = jnp.cumsum(cp, axis=1) - cp
    used = jnp.sum(cp, axis=1)
    seg_len = jnp.concatenate([cp.T, (ru - used)[None, :]], axis=0)
    seg_src = jnp.concatenate([(jnp.arange(nt)[:, None] * ru + lo).T,
                               (jnp.arange(nt) * ru + used)[None, :]], axis=0)
    seg_off = jnp.cumsum(seg_len, axis=1) - seg_len
    tot = jnp.sum(seg_len, axis=1)
    eblk = -(-tot // bu)
    bend = jnp.cumsum(eblk)
    bstart = bend - eblk
    nblk = (nt * ru) // bu + ne + 1
    blk = jnp.arange(nblk, dtype=jnp.int32)
    e = jnp.sum((bend[None, :] <= blk[:, None]).astype(jnp.int32), axis=1)
    ec = jnp.minimum(e, ne)
    w = (blk - bstart[ec])[:, None] * bu + jnp.arange(bu, dtype=jnp.int32)[None, :]
    is_pad = (e > ne)[:, None] | (w >= tot[ec][:, None])
    so = seg_off[ec]
    sv = (seg_src - seg_off)[ec]
    inside = so[:, None, :] <= w[:, :, None]
    last = inside & ~jnp.concatenate([inside[:, :, 1:], jnp.zeros_like(inside[:, :, :1])], axis=2)
    unit = (jnp.sum(jnp.where(last, sv[:, None, :], 0), axis=2) + w).astype(jnp.int32)
    dump = nt * ru + jnp.arange(bu, dtype=jnp.int32)[None, :]
    tab = jnp.concatenate([jnp.where(is_pad, 0, unit), jnp.where(is_pad, dump, unit)], axis=1)
    pad_l = -(-2 * bu // LANES) * LANES - 2 * bu
    tab = jnp.pad(tab, ((0, 0), (0, pad_l))).reshape(nblk, 1, -1).astype(jnp.int32)
    nact = bend[ne].astype(jnp.int32).reshape(1)
    return tab, e.astype(jnp.int32), nact, nblk


def _moe_ffn_body(bexp_ref, nact_ref, tab_ref, tabn_ref, xs_hbm, wup_ref, bup_ref, wdn_ref, bdn_ref,
                  out_hbm, xbuf, obuf, sem_in, sem_out, *, bu, ne):
    b = pl.program_id(0)
    nact = nact_ref[0]
    slot = b % 2
    dw = xbuf.shape[-1]

    def start_in(tref, s):
        for j in range(bu):
            pltpu.make_async_copy(xs_hbm.at[tref[0, 0, j]], xbuf.at[s, j], sem_in.at[s]).start()

    def wait_in(s):
        for j in range(bu):
            pltpu.make_async_copy(xs_hbm.at[0], xbuf.at[s, j], sem_in.at[s]).wait()

    def start_out():
        for j in range(bu):
            pltpu.make_async_copy(obuf.at[j], out_hbm.at[tab_ref[0, 0, bu + j]], sem_out).start()

    def wait_out():
        for j in range(bu):
            pltpu.make_async_copy(obuf.at[j], out_hbm.at[0], sem_out).wait()

    @pl.when(b < nact)
    def _():
        @pl.when(b == 0)
        def _():
            start_in(tab_ref, 0)

        wait_in(slot)

        @pl.when(b + 1 < nact)
        def _():
            start_in(tabn_ref, 1 - slot)

        e = bexp_ref[b]

        @pl.when(e < ne)
        def _():
            x = _unpack_pairs(xbuf[slot].reshape(bu * UNIT, dw)).astype(BF16)
            h = _dot(x, wup_ref[0]) + bup_ref[0]
            f = h.shape[1] // 2
            glu = jnp.minimum(h[:, :f], SWIGLU_LIMIT)
            lin = jnp.clip(h[:, f:], -SWIGLU_LIMIT, SWIGLU_LIMIT)
            act = glu * (1.0 / (1.0 + jnp.exp(-SWIGLU_ALPHA * glu))) * (lin + 1.0)
            out = _dot(act.astype(BF16), wdn_ref[0]) + bdn_ref[0]

            @pl.when(b > 0)
            def _():
                wait_out()

            obuf[...] = _pack_pairs(out.astype(BF16).astype(F32)).reshape(bu, UNIT, dw)

        @pl.when(e >= ne)
        def _():
            @pl.when(b > 0)
            def _():
                wait_out()

            obuf[...] = jnp.zeros(obuf.shape, jnp.uint32)

        start_out()

        @pl.when(b == nact - 1)
        def _():
            wait_out()
            nu = out_hbm.shape[0] - bu
            for j in range(bu):
                pltpu.make_async_copy(obuf.at[j], out_hbm.at[nu + j], sem_out).start()
            wait_out()


def _moe_ffn(xs, cp, wup_b, bup, wdn_b, bdn, rt):
    ne, d, f2 = wup_b.shape
    bu = FFN_ROWS // UNIT
    tab, bexp, nact, nblk = _moe_tables(cp, rt, bu)
    nu = xs.shape[0] // UNIT
    dw = xs.shape[1]
    xs3 = xs.reshape(nu, UNIT, dw)
    tl = tab.shape[-1]
    wsel = lambda b, be, na: (jnp.minimum(be[b], ne - 1), 0, 0)
    out = pl.pallas_call(
        functools.partial(_moe_ffn_body, bu=bu, ne=ne),
        grid_spec=pltpu.PrefetchScalarGridSpec(
            num_scalar_prefetch=2,
            grid=(nblk,),
            in_specs=[pl.BlockSpec((1, 1, tl), lambda b, be, na: (b, 0, 0), memory_space=pltpu.SMEM),
                      pl.BlockSpec((1, 1, tl), lambda b, be, na: (jnp.minimum(b + 1, nblk - 1), 0, 0),
                                   memory_space=pltpu.SMEM),
                      pl.BlockSpec(memory_space=pl.ANY),
                      pl.BlockSpec((1, d, f2), wsel), pl.BlockSpec((1, 1, f2), wsel),
                      pl.BlockSpec((1, f2 // 2, d), wsel), pl.BlockSpec((1, 1, d), wsel)],
            out_specs=pl.BlockSpec(memory_space=pl.ANY),
            scratch_shapes=[pltpu.VMEM((2, bu, UNIT, dw), jnp.uint32), pltpu.VMEM((bu, UNIT, dw), jnp.uint32),
                            pltpu.SemaphoreType.DMA((2,)), pltpu.SemaphoreType.DMA(())]),
        out_shape=jax.ShapeDtypeStruct((nu + bu, UNIT, dw), jnp.uint32),
        compiler_params=_cparams(("arbitrary",)),
    )(bexp, nact, tab, tab, xs3, wup_b, bup.reshape(ne, 1, f2), wdn_b, bdn.reshape(ne, 1, d))
    return out.reshape((nu + bu) * UNIT, dw)


def _moe_combine_body(o_ref, pos_ref, info_ref, h_ref, fw_ref, y_ref, *, rt):
    pos = pos_ref[...]
    info = info_ref[...]
    lane = lax.broadcasted_iota(jnp.int32, (pos.shape[0], rt), 1).astype(F32)
    selt = jnp.where(lane == pos[:, 0:1], info[:, TOP_K:TOP_K + 1], 0.0)
    for k in range(1, TOP_K):
        selt = selt + jnp.where(lane == pos[:, k:k + 1], info[:, TOP_K + k:TOP_K + k + 1], 0.0)
    y = _dot(selt.astype(BF16), _unpack_pairs(o_ref[...]).astype(BF16))
    y_ref[...] = _rms(h_ref[...] + y, fw_ref[...])


def _moe_combine(outs, pos, info, h, final_w, rt):
    t, d = h.shape
    tt = min(TOKEN_TILE, t)
    row = lambda i: (i, 0)
    return pl.pallas_call(
        functools.partial(_moe_combine_body, rt=rt),
        grid=(t // tt,),
        in_specs=[pl.BlockSpec((rt, d // 2), row), pl.BlockSpec((tt, 2 * TOP_K), row), pl.BlockSpec((tt, 2 * TOP_K), row),
                  pl.BlockSpec((tt, d), row), pl.BlockSpec((1, d), lambda i: (0, 0))],
        out_specs=pl.BlockSpec((tt, d), row),
        out_shape=jax.ShapeDtypeStruct((t, d), F32),
        compiler_params=_cparams(("parallel",)),
    )(outs, pos, info, h, final_w.reshape(1, d))


def _group(x3, h0_re, h0_im, past, tc, wts):
    n, l, d = x3.shape
    x = x3.reshape(n * l, d)
    sw, aw = wts['sw'], wts['aw']
    hd = aw // (2 * ATT_HEADS)
    prompt = past is None
    proj = _inproj(x, wts['attn_norm_w'], wts['w_in'], sw, aw, hd ** -0.5, pack=prompt)
    u, k, v, qb, kb = proj[:5]
    if prompt:
        vt, u2 = proj[5:]
        ys, st_re, st_im = _s5(u2, h0_re, h0_im, wts['s5'][tc], n, l, tc, packed=True)
        o = _attn_prompt(qb, kb, vt, wts['lam'], wts['subln_w'], n, l, wts['post'])
    else:
        ys, st_re, st_im = _s5(u, h0_re, h0_im, wts['s5'][tc], n, l, tc, packed=False)
        o = _attn_sample(qb, kb, proj[5], past[0], past[1], past[2], wts['lam'], wts['subln_w'], n, l, wts['post'])
    h, xn, info, infot = _mix(ys, u, wts['d_skip'], o, x, wts['glu_w'], wts['glu_b'], wts['ssm_norm_w'],
                              wts['w_out'], wts['ffn_norm_w'], wts['router_w'], wts['router_b'], packed=prompt)
    ne = wts['w_up'].shape[0]
    rt = _seg_rows(infot.shape[2], ne)
    xs, cpl, pos = _moe_sort(xn, infot, ne)
    outs = _moe_ffn(xs, cpl[:, :, 0], wts['w_up'], wts['b_up'], wts['w_down'], wts['b_down'], rt)
    y = _moe_combine(outs, pos, info, h, wts['final_norm_w'], rt)
    heads = (n, l, ATT_HEADS)
    return (y.reshape(n, l, d), k.reshape(heads + (2, hd))[None], v.reshape(heads + (2 * hd,))[None],
            st_re[None], st_im[None])


def kernel(x_prompt, x_sample, cache_k, cache_v, state_ssm_re, state_ssm_im, page_table, attn_norm_w, w_in, ssm_lambda_re, ssm_lambda_im, ssm_log_dt, ssm_b_re, ssm_b_im, ssm_c_re, ssm_c_im, ssm_d, ssm_glu_w, ssm_glu_b, ssm_norm_w, diff_lambda_q1, diff_lambda_k1, diff_lambda_q2, diff_lambda_k2, subln_w, w_out, ffn_norm_w, router_w, router_b, w_up, b_up, w_down, b_down, final_norm_w):
    assert w_in.shape[0] == 1, "single-layer trunk"
    g = ssm_lambda_re.shape[1]
    sw = g * SSM_GROUP
    aw = (w_in.shape[-1] - sw) // 3
    ne = router_w.shape[-1]
    lam_init = 0.8 - 0.6 * math.exp(-0.3 * 0)
    lam = (jnp.exp(jnp.sum(diff_lambda_q1[0] * diff_lambda_k1[0]))
           - jnp.exp(jnp.sum(diff_lambda_q2[0] * diff_lambda_k2[0])) + lam_init).astype(F32)
    t_dec = x_sample.shape[1]
    s5_args = (ssm_lambda_re[0], ssm_lambda_im[0], ssm_log_dt[0], ssm_b_re[0], ssm_b_im[0],
               ssm_c_re[0], ssm_c_im[0])
    wts = {
        'sw': sw, 'aw': aw, 'lam': lam, 'post': 1.0 - lam_init,
        'attn_norm_w': attn_norm_w[0], 'w_in': w_in[0].astype(BF16),
        's5': {tc: _s5_params(*s5_args, tc) for tc in {S5_CHUNK, t_dec}}, 'd_skip': ssm_d[0],
        'glu_w': ssm_glu_w[0].astype(BF16), 'glu_b': ssm_glu_b[0], 'ssm_norm_w': ssm_norm_w[0],
        'subln_w': subln_w[0], 'w_out': w_out[0].astype(BF16), 'ffn_norm_w': ffn_norm_w[0],
        'router_w': jnp.pad(router_w[0], ((0, 0), (0, LANES - ne))).astype(BF16),
        'router_b': jnp.pad(router_b[0], (0, LANES - ne), constant_values=NEG_BIG).reshape(1, LANES),
        'w_up': _wperm(w_up[0]),
        'b_up': jnp.concatenate([b_up[0, :, 0::2], b_up[0, :, 1::2]], axis=-1),
        'w_down': w_down[0].astype(BF16), 'b_down': b_down[0], 'final_norm_w': final_norm_w,
    }
    nb = x_prompt.shape[0]
    zeros = jnp.zeros((nb, g, SSM_STATE), F32)
    yp, kp, vp, srp, sip = _group(x_prompt, zeros, zeros, None, S5_CHUNK, wts)
    ys, ks, vs, srs, sis = _group(x_sample, state_ssm_re[0], state_ssm_im[0],
                                  (cache_k[0], cache_v[0], page_table), t_dec, wts)
    return (yp, ys, kp, vp, srp, sip, ks, vs, srs, sis)
```

```python
import functools
import math

import jax
import jax.numpy as jnp
from jax import lax
from jax.experimental import pallas as pl
from jax.experimental.pallas import tpu as pltpu

F32 = jnp.float32
BF16 = jnp.bfloat16

SSM_GROUP = 16
SSM_STATE = 64
ATT_HEADS = 4
TOP_K = 4
SWIGLU_LIMIT = 7.0
SWIGLU_ALPHA = 1.702
RMS_EPS = 1e-6
NEG_BIG = -1e30

LANES = 128
SUBLANES = 8
VMEM_LIMIT = 56 * 1024 * 1024

INPROJ_ROWS = 512
ATTN_Q = 256
ATTN_K = 512
S5_CHUNK = 16
S5_ROWS = 512
TOKEN_TILE = 256
FFN_ROWS = 512
UNIT = SUBLANES
DEC_PAGES = 8


def _cparams(sem):
    return pltpu.CompilerParams(dimension_semantics=sem, vmem_limit_bytes=VMEM_LIMIT)


def _rms(x, w):
    ms = jnp.mean(x * x, axis=-1, keepdims=True)
    return x * lax.rsqrt(ms + RMS_EPS) * w


def _dot(a, b):
    return jnp.dot(a, b, preferred_element_type=F32)


def _dot_nt(a, b):
    return lax.dot_general(a, b, (((1,), (1,)), ((), ())), preferred_element_type=F32)


def _pack_pairs(x):
    half = x.shape[1] // 2
    lo = lax.bitcast_convert_type(x[:, :half], jnp.uint32)
    hi = lax.bitcast_convert_type(x[:, half:], jnp.uint32)
    return lax.shift_right_logical(lo, jnp.uint32(16)) | (hi & jnp.uint32(0xFFFF0000))


def _unpack_pairs(w):
    lo = lax.bitcast_convert_type(lax.shift_left(w, jnp.uint32(16)), F32)
    hi = lax.bitcast_convert_type(w & jnp.uint32(0xFFFF0000), F32)
    return jnp.concatenate([lo, hi], axis=1)


def _lane_mask(shape, lo, width):
    lane = lax.broadcasted_iota(jnp.int32, shape, 1)
    return (lane >= lo) & (lane < lo + width)


def _inproj_body(x_ref, nw_ref, w_ref, *refs, sw, aw, scale, pack):
    xn = _rms(x_ref[...], nw_ref[...])
    proj = _dot(xn.astype(BF16), w_ref[...])
    q = proj[:, sw:sw + aw]
    k = proj[:, sw + aw:sw + 2 * aw]
    v = proj[:, sw + 2 * aw:]
    if not pack:
        u_ref, k_ref, v_ref, qb_ref, kb_ref, vb_ref = refs
        k_ref[...] = k
        v_ref[...] = v
        vb_ref[...] = v.astype(BF16)
    else:
        u_ref, kt_ref, v4_ref, qb_ref, kb_ref, vt_ref, u2_ref, u_s = refs
        for vt in range(sw // LANES):
            u_s[vt] = proj[:, vt * LANES:(vt + 1) * LANES]
        vtr = v.T
        vt_ref[0] = vtr.astype(BF16)
        kt_ref[0] = k.T
        vd = aw // ATT_HEADS
        for h in range(ATT_HEADS):
            v4_ref[pl.ds(h, v.shape[0], stride=ATT_HEADS), :] = v[:, h * vd:(h + 1) * vd]
    u_ref[...] = proj[:, :sw]
    qb_ref[...] = (q * scale).astype(BF16)
    kb_ref[...] = k.astype(BF16)
    if pack:
        nk = u_s.shape[1] // S5_CHUNK
        per = LANES // SSM_GROUP
        for vt in range(sw // LANES):
            for half in range(S5_CHUNK // per):
                accs = [None] * per
                for s8 in range(per):
                    xv = u_s[vt, pl.ds(half * per + s8, nk, stride=S5_CHUNK), :]
                    mask = _lane_mask(xv.shape, s8 * SSM_GROUP, SSM_GROUP)
                    for gp in range(per):
                        shift = (SSM_GROUP * (s8 - gp)) % LANES
                        r = pltpu.roll(xv, shift, axis=1) if shift else xv
                        accs[gp] = r if s8 == 0 else jnp.where(mask, r, accs[gp])
                for gp in range(per):
                    u2_ref[vt * per + gp, :, half * LANES:(half + 1) * LANES] = accs[gp].astype(BF16)


def _inproj(x, norm_w, w_in_b, sw, aw, scale, pack, l=None):
    t, d = x.shape
    tm = min(INPROJ_ROWS, t)
    row = lambda i: (i, 0)
    fix = lambda i: (0, 0)
    f32o = lambda w: jax.ShapeDtypeStruct((t, w), F32)
    b16o = lambda w: jax.ShapeDtypeStruct((t, w), BF16)
    if pack:
        per = l // tm
        vd = aw // ATT_HEADS
        kv_specs = [pl.BlockSpec((1, aw, tm), lambda i: (i // per, 0, i % per)),
                    pl.BlockSpec((tm * ATT_HEADS, vd), row)]
        kv_shape = [jax.ShapeDtypeStruct((t // l, aw, l), F32), jax.ShapeDtypeStruct((t * ATT_HEADS, vd), F32)]
    else:
        kv_specs = [pl.BlockSpec((tm, aw), row)] * 2
        kv_shape = [f32o(aw), f32o(aw)]
    out_specs = [pl.BlockSpec((tm, sw), row)] + kv_specs + [pl.BlockSpec((tm, aw), row)] * 2
    out_shape = [f32o(sw)] + kv_shape + [b16o(aw), b16o(aw)]
    scratch = []
    if pack:
        g = sw // SSM_GROUP
        kd = S5_CHUNK * SSM_GROUP
        out_specs += [pl.BlockSpec((1, aw, tm), lambda i: (i, 0, 0)),
                      pl.BlockSpec((g, tm // S5_CHUNK, kd), lambda i: (0, i, 0))]
        out_shape += [jax.ShapeDtypeStruct((t // tm, aw, tm), BF16),
                      jax.ShapeDtypeStruct((g, t // S5_CHUNK, kd), BF16)]
        scratch = [pltpu.VMEM((sw // LANES, tm, LANES), F32)]
    else:
        out_specs += [pl.BlockSpec((tm, aw), row)]
        out_shape += [b16o(aw)]
    return pl.pallas_call(
        functools.partial(_inproj_body, sw=sw, aw=aw, scale=scale, pack=pack),
        grid=(t // tm,),
        in_specs=[pl.BlockSpec((tm, d), row), pl.BlockSpec((1, d), fix), pl.BlockSpec(w_in_b.shape, fix)],
        out_specs=out_specs, out_shape=out_shape, scratch_shapes=scratch,
        compiler_params=_cparams(("parallel",)),
    )(x, norm_w.reshape(1, d), w_in_b)


def _s5_params(lam_re, lam_im, log_dt, b_re, b_im, c_re, c_im, tc):
    hp = lax.Precision.HIGHEST
    g, p = lam_re.shape
    c = b_re.shape[-1]
    dt = jnp.exp(log_dt)[:, None]
    mag = jnp.exp(lam_re * dt)
    a_re = mag * jnp.cos(lam_im * dt)
    a_im = mag * jnp.sin(lam_im * dt)
    den = lam_re * lam_re + lam_im * lam_im
    nr = a_re - 1.0
    cf_re = (nr * lam_re + a_im * lam_im) / den
    cf_im = (a_im * lam_re - nr * lam_im) / den
    bb_re = cf_re[..., None] * b_re - cf_im[..., None] * b_im
    bb_im = cf_re[..., None] * b_im + cf_im[..., None] * b_re
    pr, pi = [jnp.ones_like(a_re)], [jnp.zeros_like(a_re)]
    for _ in range(tc):
        pr, pi = pr + [pr[-1] * a_re - pi[-1] * a_im], pi + [pr[-1] * a_im + pi[-1] * a_re]
    pr = jnp.stack(pr)
    pi = jnp.stack(pi)
    ab_re = pr[:tc, :, :, None] * bb_re - pi[:tc, :, :, None] * bb_im
    ab_im = pr[:tc, :, :, None] * bb_im + pi[:tc, :, :, None] * bb_re
    kern = (jnp.einsum('gdp,tgpc->tgdc', c_re, ab_re, precision=hp)
            - jnp.einsum('gdp,tgpc->tgdc', c_im, ab_im, precision=hp))
    lag = jnp.arange(tc)[None, :] - jnp.arange(tc)[:, None]
    kt = kern[jnp.clip(lag, 0, tc - 1)]
    kt = jnp.where((lag >= 0)[:, :, None, None, None], kt, 0.0)
    m = kt.transpose(2, 0, 4, 1, 3).reshape(g, tc * c, tc * c)
    wr = ab_re[::-1].transpose(1, 0, 3, 2).reshape(g, tc * c, p)
    wi = ab_im[::-1].transpose(1, 0, 3, 2).reshape(g, tc * c, p)
    w = jnp.concatenate([wr, wi], axis=-1)
    wsw = jnp.concatenate([wi, wr], axis=-1)
    e_re = c_re[None] * pr[1:, :, None, :] - c_im[None] * pi[1:, :, None, :]
    e_im = c_re[None] * pi[1:, :, None, :] + c_im[None] * pr[1:, :, None, :]
    v = jnp.concatenate([e_re.transpose(1, 3, 0, 2).reshape(g, p, tc * c),
                         -e_im.transpose(1, 3, 0, 2).reshape(g, p, tc * c)], axis=1)
    ar = jnp.concatenate([pr[tc], pr[tc]], axis=-1)[:, None, :]
    ai = jnp.concatenate([-pi[tc], pi[tc]], axis=-1)[:, None, :]
    return m.astype(BF16), w.astype(BF16), wsw.astype(BF16), v.astype(BF16), ar, ai


def _s5_body(u_ref, m_ref, w_ref, wsw_ref, v_ref, ar_ref, ai_ref, h0_ref, h0s_ref,
             y_ref, hout_ref, h_s, hs_s, s_s, ssw_s, hin_s, *, cb, nb, seq_major):
    @pl.when(pl.program_id(1) == 0)
    def _():
        h_s[...] = h0_ref[0]
        hs_s[...] = h0s_ref[0]

    kd = u_ref.shape[-1]
    ub = u_ref[0].reshape(cb * nb, kd).astype(BF16)
    yi = _dot(ub, m_ref[0])
    s_s[...] = _dot(ub, w_ref[0])
    ssw_s[...] = _dot(ub, wsw_ref[0])
    ar = ar_ref[0]
    ai = ai_ref[0]
    h = h_s[...]
    hs = hs_s[...]
    for c in range(cb):
        idx = pl.ds(c, nb, stride=cb) if (seq_major and cb > 1) else pl.ds(c * nb, nb)
        hin_s[idx, :] = h
        h, hs = ar * h + ai * hs + s_s[idx, :], ar * hs - ai * h + ssw_s[idx, :]
    h_s[...] = h
    hs_s[...] = hs
    hout_ref[0] = h
    yo = _dot(hin_s[...].astype(BF16), v_ref[0])
    y_ref[0] = (yi + yo).reshape(y_ref.shape[1:])


def _s5(u_in, h0_re, h0_im, prm, n, l, tc, packed):
    m, w, wsw, v, ar, ai = prm
    g = m.shape[0]
    c = SSM_GROUP
    p2 = 2 * SSM_STATE
    nc = l // tc
    kd = tc * c
    cb = max(1, min(nc, S5_ROWS // n))
    assert nc % cb == 0 and n % SUBLANES == 0
    if packed:
        u2 = u_in.reshape(g, n, nc, kd)
        blk_shape, blk = (1, n, cb, kd), (lambda gi, j: (gi, 0, j, 0))
    else:
        u2 = u_in.reshape(n, nc, tc, g, c).transpose(3, 1, 0, 2, 4).reshape(g, nc, n, kd).astype(BF16)
        blk_shape, blk = (1, cb, n, kd), (lambda gi, j: (gi, j, 0, 0))
    h0 = jnp.concatenate([h0_re, h0_im], axis=-1).transpose(1, 0, 2)
    h0s = jnp.concatenate([h0_im, h0_re], axis=-1).transpose(1, 0, 2)
    par = lambda gi, j: (gi, 0, 0)
    y2, hout = pl.pallas_call(
        functools.partial(_s5_body, cb=cb, nb=n, seq_major=packed),
        grid=(g, nc // cb),
        in_specs=[pl.BlockSpec(blk_shape, blk),
                  pl.BlockSpec((1, kd, kd), par), pl.BlockSpec((1, kd, p2), par), pl.BlockSpec((1, kd, p2), par),
                  pl.BlockSpec((1, p2, kd), par),
                  pl.BlockSpec((1, 1, p2), par), pl.BlockSpec((1, 1, p2), par),
                  pl.BlockSpec((1, n, p2), par), pl.BlockSpec((1, n, p2), par)],
        out_specs=[pl.BlockSpec(blk_shape, blk), pl.BlockSpec((1, n, p2), par)],
        out_shape=[jax.ShapeDtypeStruct(u2.shape, F32), jax.ShapeDtypeStruct((g, n, p2), F32)],
        scratch_shapes=[pltpu.VMEM((n, p2), F32), pltpu.VMEM((n, p2), F32), pltpu.VMEM((cb * n, p2), F32),
                        pltpu.VMEM((cb * n, p2), F32), pltpu.VMEM((cb * n, p2), F32)],
        compiler_params=_cparams(("parallel", "arbitrary")),
    )(u2, m, w, wsw, v, ar, ai, h0, h0s)
    if packed:
        y = y2.reshape(g, n * nc, kd)
    else:
        y = y2.reshape(g, nc, n, tc, c).transpose(2, 1, 3, 0, 4).reshape(n * l, g * c)
    hout = hout.transpose(1, 0, 2)
    return y, hout[..., :SSM_STATE], hout[..., SSM_STATE:]


def _attn_body(lam_ref, q_ref, k_ref, vt_ref, swb_ref, o_ref, q2_s, m_s, acc_s, *, tq, tk, vd, post):
    qi = pl.program_id(2)
    hd = vd // 2
    q = q_ref[0]
    lane = lax.broadcasted_iota(jnp.int32, q.shape, 1)
    zero = jnp.zeros_like(q)
    q2_s[0:tq, :] = jnp.where(lane < hd, q, zero)
    q2_s[tq:, :] = jnp.where(lane >= hd, q, zero)
    m_s[...] = jnp.full(m_s.shape, NEG_BIG, F32)
    acc_s[...] = jnp.zeros(acc_s.shape, F32)
    ones = jnp.ones((vd, tk), BF16)

    def scores(j):
        start = pl.multiple_of(j * tk, tk)
        return _dot_nt(k_ref[0, pl.ds(start, tk), :], q2_s[...])

    def step(j, st2, off):
        vaug = jnp.concatenate([vt_ref[j], ones], axis=0)
        for c in range(2):
            cs = slice(c * tq, (c + 1) * tq)
            st = st2[:, cs]
            if off is not None:
                key = lax.broadcasted_iota(jnp.int32, st.shape, 0)
                row = lax.broadcasted_iota(jnp.int32, st.shape, 1)
                st = jnp.where(key <= row + off, st, NEG_BIG)
            m_old = m_s[:, cs]
            m_new = jnp.maximum(m_old, jnp.max(st, axis=0, keepdims=True))
            alpha = jnp.exp(m_old - m_new)
            p = jnp.exp(st - m_new).astype(BF16)
            acc_s[:, cs] = alpha * acc_s[:, cs] + _dot(vaug, p)
            m_s[:, cs] = m_new

    def full_step(j, st2):
        nxt = scores(j + 1)
        step(j, st2, None)
        return nxt

    nfull = (qi * tq) // tk
    st2 = lax.fori_loop(0, nfull, full_step, scores(0))
    step(nfull, st2, qi * tq - nfull * tk)
    acc = acc_s[...]
    ot = (acc[:vd, :tq] / acc[vd:vd + 1, :tq]) - lam_ref[0] * (acc[:vd, tq:] / acc[vd:vd + 1, tq:])
    ms = jnp.mean(ot * ot, axis=0, keepdims=True)
    ot = ot * lax.rsqrt(ms + RMS_EPS) * swb_ref[...] * post
    o_ref[0] = ot.T.astype(o_ref.dtype)


def _attn_prompt(qb, kb, vt, lam, subln_w, n, l, post):
    aw = qb.shape[-1]
    vd = aw // ATT_HEADS
    tk = vt.shape[-1]
    tq = min(ATTN_Q, l)
    assert l % tk == 0 and tk % tq == 0
    q3, k3 = qb.reshape(n, l, aw), kb.reshape(n, l, aw)
    swb = jnp.broadcast_to(subln_w.reshape(vd, 1), (vd, tq))
    return pl.pallas_call(
        functools.partial(_attn_body, tq=tq, tk=tk, vd=vd, post=post),
        grid=(n, ATT_HEADS, l // tq),
        in_specs=[pl.BlockSpec(memory_space=pltpu.SMEM),
                  pl.BlockSpec((1, tq, vd), lambda b, h, i: (b, i, h)),
                  pl.BlockSpec((1, l, vd), lambda b, h, i: (b, 0, h)),
                  pl.BlockSpec((l // tk, vd, tk), lambda b, h, i: (b, h, 0)),
                  pl.BlockSpec((vd, tq), lambda b, h, i: (0, 0))],
        out_specs=pl.BlockSpec((1, tq, vd), lambda b, h, i: (b, i, h)),
        out_shape=jax.ShapeDtypeStruct((n, l, aw), BF16),
        scratch_shapes=[pltpu.VMEM((2 * tq, vd), BF16), pltpu.VMEM((1, 2 * tq), F32),
                        pltpu.VMEM((2 * vd, 2 * tq), F32)],
        compiler_params=_cparams(("parallel", "parallel", "arbitrary")),
    )(lam.reshape(1), q3, k3, vt, swb).reshape(n * l, aw)


def _attn_dec_body(pt_ref, lam_ref, qr_ref, kn_ref, vn_ref, sw_ref, *rest, pg, t_new, vd, post):
    k_refs, v_refs = rest[:pg], rest[pg:2 * pg]
    o_ref, m_s, l_s, acc_s = rest[2 * pg:]
    j = pl.program_id(1)

    @pl.when(j == 0)
    def _():
        m_s[...] = jnp.full(m_s.shape, NEG_BIG, F32)
        l_s[...] = jnp.zeros(l_s.shape, F32)
        acc_s[...] = jnp.zeros(acc_s.shape, F32)

    qr = qr_ref[0]
    rows = 2 * t_new
    page = k_refs[0].shape[2]

    def update(s, vhs):
        m_old = m_s[...]
        m_new = jnp.maximum(m_old, jnp.max(s, axis=-1, keepdims=True))
        alpha = jnp.exp(m_old - m_new)
        p = jnp.exp(s - m_new).astype(BF16)
        l_s[...] = alpha * l_s[...] + jnp.sum(p.astype(F32), axis=-1, keepdims=True)
        for h in range(ATT_HEADS):
            hs = slice(h * rows, (h + 1) * rows)
            pv = _dot(p[hs, :page], vhs[h][0])
            for i in range(1, len(vhs[h])):
                pv = pv + _dot(p[hs, i * page:(i + 1) * page], vhs[h][i])
            acc_s[hs, :] = alpha[hs] * acc_s[hs, :] + pv
        m_s[...] = m_new

    s = jnp.concatenate([_dot(qr, kr[0].astype(BF16)) for kr in k_refs], axis=1)
    update(s, [[vr[0, pl.ds(h, page, stride=ATT_HEADS), :].astype(BF16) for vr in v_refs]
               for h in range(ATT_HEADS)])

    @pl.when(j == pl.num_programs(1) - 1)
    def _():
        sn = _dot_nt(qr, kn_ref[0])
        r = lax.broadcasted_iota(jnp.int32, sn.shape, 0) % t_new
        c = lax.broadcasted_iota(jnp.int32, sn.shape, 1)
        vn = vn_ref[0]
        update(jnp.where(c <= r, sn, NEG_BIG), [[vn[:, h * vd:(h + 1) * vd]] for h in range(ATT_HEADS)])
        acc = acc_s[...]
        l = l_s[...]
        for h in range(ATT_HEADS):
            blk = acc[h * rows:(h + 1) * rows] / l[h * rows:(h + 1) * rows]
            o = blk[:t_new] - lam_ref[0] * blk[t_new:]
            o_ref[0, :, h * vd:(h + 1) * vd] = _rms(o, sw_ref[...]) * post


def _attn_sample(qb, kb, vb, cache_k, cache_v, page_table, lam, subln_w, nseq, t_new, post):
    aw = qb.shape[-1]
    vd = aw // ATT_HEADS
    hd = vd // 2
    n_pool, page = cache_k.shape[0], cache_k.shape[1]
    n_pages = page_table.shape[1]
    pg = math.gcd(DEC_PAGES, n_pages)
    ck = cache_k.reshape(n_pool, page, aw).transpose(0, 2, 1)
    cv = cache_v.reshape(n_pool, page * ATT_HEADS, vd)
    q5 = qb.reshape(nseq, t_new, ATT_HEADS, 2, hd)
    qr = jnp.einsum('bqhcd,hx,cy->bhcqxyd', q5, jnp.eye(ATT_HEADS, dtype=BF16), jnp.eye(2, dtype=BF16))
    nr = ATT_HEADS * 2 * t_new
    qr = qr.reshape(nseq, nr, aw)
    pad = ((0, 0), (0, page - t_new), (0, 0))
    kn = jnp.pad(kb.reshape(nseq, t_new, aw), pad)
    vn = jnp.pad(vb.reshape(nseq, t_new, aw), pad)
    seq = lambda b, j, pt: (b, 0, 0)
    pick = lambda b, j, pt, i: (pt[b, j * pg + i], 0, 0)
    k_specs = [pl.BlockSpec((1, aw, page), functools.partial(pick, i=i)) for i in range(pg)]
    v_specs = [pl.BlockSpec((1, page * ATT_HEADS, vd), functools.partial(pick, i=i)) for i in range(pg)]
    return pl.pallas_call(
        functools.partial(_attn_dec_body, pg=pg, t_new=t_new, vd=vd, post=post),
        grid_spec=pltpu.PrefetchScalarGridSpec(
            num_scalar_prefetch=1,
            grid=(nseq, n_pages // pg),
            in_specs=[pl.BlockSpec(memory_space=pltpu.SMEM),
                      pl.BlockSpec((1, nr, aw), seq), pl.BlockSpec((1, page, aw), seq),
                      pl.BlockSpec((1, page, aw), seq), pl.BlockSpec((1, vd), lambda b, j, pt: (0, 0))]
            + k_specs + v_specs,
            out_specs=pl.BlockSpec((1, t_new, aw), seq),
            scratch_shapes=[pltpu.VMEM((nr, 1), F32), pltpu.VMEM((nr, 1), F32), pltpu.VMEM((nr, vd), F32)]),
        out_shape=jax.ShapeDtypeStruct((nseq, t_new, aw), F32),
        compiler_params=_cparams(("parallel", "arbitrary")),
    )(page_table, lam.reshape(1), qr, kn, vn, subln_w.reshape(1, vd), *([ck] * pg), *([cv] * pg)
      ).reshape(nseq * t_new, aw)


def _mix_body(ys_ref, u_ref, dsk_ref, o_ref, x_ref, gw_ref, gb_ref, snw_ref, wo_ref, fnw_ref, rw_ref, rb_ref,
              h_ref, xn_ref, info_ref, infot_ref, *scratch, sw, packed):
    if packed:
        ys_s, = scratch
        nk = ys_s.shape[1] // S5_CHUNK
        per = LANES // SSM_GROUP
        for vt in range(sw // LANES):
            for half in range(S5_CHUNK // per):
                srcs = [ys_ref[vt * per + gp, :, half * LANES:(half + 1) * LANES] for gp in range(per)]
                masks = [_lane_mask(srcs[0].shape, gp * SSM_GROUP, SSM_GROUP) for gp in range(per)]
                for s8 in range(per):
                    val = None
                    for gp in range(per):
                        shift = (SSM_GROUP * (gp - s8)) % LANES
                        r = pltpu.roll(srcs[gp], shift, axis=1) if shift else srcs[gp]
                        val = r if gp == 0 else jnp.where(masks[gp], r, val)
                    ys_s[vt, pl.ds(half * per + s8, nk, stride=S5_CHUNK), :] = val
        ys = jnp.concatenate([ys_s[vt] for vt in range(sw // LANES)], axis=1)
    else:
        ys = ys_ref[...]
    y = jax.nn.gelu(ys + dsk_ref[...] * u_ref[...])
    z = _dot(y.astype(BF16), gw_ref[...]) + gb_ref[...]
    y = y * (1.0 / (1.0 + jnp.exp(-z)))
    yn = _rms(y, snw_ref[...])
    mixed = _dot(yn.astype(BF16), wo_ref[:sw, :]) + _dot(o_ref[...].astype(BF16), wo_ref[sw:, :])
    h = x_ref[...] + mixed
    h_ref[...] = h
    xnb = _rms(h, fnw_ref[...]).astype(BF16)
    xn_ref[...] = xnb
    logits = _dot(xnb, rw_ref[...]) + rb_ref[...]
    lane = lax.broadcasted_iota(jnp.int32, logits.shape, 1).astype(F32)
    work = logits
    vals, idxs = [], []
    for _ in range(TOP_K):
        mx = jnp.max(work, axis=-1, keepdims=True)
        ix = jnp.min(jnp.where(work == mx, lane, float(LANES)), axis=-1, keepdims=True)
        vals.append(mx)
        idxs.append(ix)
        work = jnp.where(lane == ix, -3e38, work)
    es = [jnp.exp(v - vals[0]) for v in vals]
    den = es[0] + es[1] + es[2] + es[3]
    info = jnp.zeros(logits.shape, F32)
    for k in range(TOP_K):
        info = info + jnp.where(lane == float(k), idxs[k], 0.0) + jnp.where(lane == float(TOP_K + k), es[k] / den, 0.0)
    info_ref[...] = info[:, :2 * TOP_K]
    infot_ref[0] = info.T[:2 * TOP_K, :]


def _mix(ys, u, d_skip, o, x, glu_w_b, glu_b, ssm_norm_w, w_out_b, ffn_norm_w, router_w_b, router_b_p, packed):
    t, d = x.shape
    sw = u.shape[1]
    aw = o.shape[1]
    tm = min(TOKEN_TILE, t)
    nt = t // tm
    row = lambda i: (i, 0)
    fix = lambda i: (0, 0)
    if packed:
        ys_spec = pl.BlockSpec((ys.shape[0], tm // S5_CHUNK, ys.shape[2]), lambda i: (0, i, 0))
        scratch = [pltpu.VMEM((sw // LANES, tm, LANES), F32)]
    else:
        ys_spec = pl.BlockSpec((tm, sw), row)
        scratch = []
    return pl.pallas_call(
        functools.partial(_mix_body, sw=sw, packed=packed),
        grid=(nt,),
        in_specs=[ys_spec, pl.BlockSpec((tm, sw), row), pl.BlockSpec((1, sw), fix),
                  pl.BlockSpec((tm, aw), row), pl.BlockSpec((tm, d), row),
                  pl.BlockSpec((sw, sw), fix), pl.BlockSpec((1, sw), fix), pl.BlockSpec((1, sw), fix),
                  pl.BlockSpec((d, d), fix), pl.BlockSpec((1, d), fix),
                  pl.BlockSpec((d, LANES), fix), pl.BlockSpec((1, LANES), fix)],
        out_specs=[pl.BlockSpec((tm, d), row), pl.BlockSpec((tm, d), row),
                   pl.BlockSpec((tm, 2 * TOP_K), row), pl.BlockSpec((1, 2 * TOP_K, tm), lambda i: (i, 0, 0))],
        out_shape=[jax.ShapeDtypeStruct((t, d), F32), jax.ShapeDtypeStruct((t, d), BF16),
                   jax.ShapeDtypeStruct((t, 2 * TOP_K), F32), jax.ShapeDtypeStruct((nt, 2 * TOP_K, tm), F32)],
        scratch_shapes=scratch,
        compiler_params=_cparams(("parallel",)),
    )(ys, u, d_skip.reshape(1, sw), o, x, glu_w_b, glu_b.reshape(1, sw), ssm_norm_w.reshape(1, sw), w_out_b,
      ffn_norm_w.reshape(1, d), router_w_b, router_b_p)


def _wperm_body(w_ref, p_ref, o_ref):
    f = o_ref.shape[-1] // 2
    grp = 2 * LANES
    for k in range(w_ref.shape[-1] // grp):
        r = _dot(w_ref[0, :, k * grp:(k + 1) * grp].astype(BF16), p_ref[...]).astype(BF16)
        o_ref[0, :, k * LANES:(k + 1) * LANES] = r[:, :LANES]
        o_ref[0, :, f + k * LANES:f + (k + 1) * LANES] = r[:, LANES:]


def _wperm(w_up):
    ne, d, f2 = w_up.shape
    grp = 2 * LANES
    src = jnp.arange(grp)[:, None]
    dst = jnp.arange(grp)[None, :]
    perm = (dst == (src % 2) * LANES + src // 2).astype(BF16)
    return pl.pallas_call(
        _wperm_body,
        grid=(ne,),
        in_specs=[pl.BlockSpec((1, d, f2), lambda e: (e, 0, 0)), pl.BlockSpec((grp, grp), lambda e: (0, 0))],
        out_specs=pl.BlockSpec((1, d, f2), lambda e: (e, 0, 0)),
        out_shape=jax.ShapeDtypeStruct((ne, d, f2), BF16),
        compiler_params=_cparams(("parallel",)),
    )(w_up, perm)


def _seg_rows(tt, ne):
    worst = tt * TOP_K + ne * (UNIT - 1)
    return -(-worst // LANES) * LANES


def _moe_sort_body(x_ref, infot_ref, xs_ref, cp_ref, pos_ref, *, tt, ne, rt):
    infot = infot_ref[0]
    sub = lax.broadcasted_iota(jnp.int32, (ne, tt), 0).astype(F32)
    onehots = [(sub == infot[k:k + 1, :]).astype(F32) for k in range(TOP_K)]
    et = onehots[0] + onehots[1] + onehots[2] + onehots[3]
    cnt = jnp.sum(et, axis=1, keepdims=True)
    cp = jnp.floor((cnt + (UNIT - 1.0)) * (1.0 / UNIT))
    er = lax.broadcasted_iota(jnp.int32, (ne, ne), 0)
    ec = lax.broadcasted_iota(jnp.int32, (ne, ne), 1)
    cpb = jnp.broadcast_to(cp, (ne, LANES))
    lo = _dot((ec < er).astype(BF16), cpb.astype(BF16))[:, :1] * float(UNIT)
    tr = lax.broadcasted_iota(jnp.int32, (tt, tt), 0)
    tc = lax.broadcasted_iota(jnp.int32, (tt, tt), 1)
    rank = _dot(et.astype(BF16), (tr < tc).astype(BF16))
    base = lo + rank
    poss = [jnp.sum(oh * base, axis=0, keepdims=True) for oh in onehots]
    rows = lax.broadcasted_iota(jnp.int32, (rt, tt), 0).astype(F32)
    sel = (rows == poss[0]).astype(F32)
    for k in range(1, TOP_K):
        sel = sel + (rows == poss[k]).astype(F32)
    xs_ref[...] = _pack_pairs(_dot(sel.astype(BF16), x_ref[...]))
    cp_ref[0] = cpb.astype(jnp.int32)
    pt = jnp.concatenate(poss + [jnp.zeros((LANES - TOP_K, tt), F32)], axis=0)
    pos_ref[...] = pt.T[:, :2 * TOP_K]


def _moe_sort(xn, infot, ne):
    t, d = xn.shape
    nt, _, tt = infot.shape
    rt = _seg_rows(tt, ne)
    return pl.pallas_call(
        functools.partial(_moe_sort_body, tt=tt, ne=ne, rt=rt),
        grid=(nt,),
        in_specs=[pl.BlockSpec((tt, d), lambda i: (i, 0)), pl.BlockSpec((1, 2 * TOP_K, tt), lambda i: (i, 0, 0))],
        out_specs=[pl.BlockSpec((rt, d // 2), lambda i: (i, 0)), pl.BlockSpec((1, ne, LANES), lambda i: (i, 0, 0)),
                   pl.BlockSpec((tt, 2 * TOP_K), lambda i: (i, 0))],
        out_shape=[jax.ShapeDtypeStruct((nt * rt, d // 2), jnp.uint32),
                   jax.ShapeDtypeStruct((nt, ne, LANES), jnp.int32),
                   jax.ShapeDtypeStruct((t, 2 * TOP_K), F32)],
        compiler_params=_cparams(("parallel",)),
    )(xn, infot)


def _moe_tables(cp, rt, bu):
    nt, ne = cp.shape
    ru = rt // UNIT
    lo = jnp.cumsum(cp, axis=1) - cp
    used = jnp.sum(cp, axis=1)
    seg_len = jnp.concatenate([cp.T, (ru - used)[None, :]], axis=0)
    seg_src = jnp.concatenate([(jnp.arange(nt)[:, None] * ru + lo).T,
                               (jnp.arange(nt) * ru + used)[None, :]], axis=0)
    seg_off = jnp.cumsum(seg_len, axis=1) - seg_len
    tot = jnp.sum(seg_len, axis=1)
    eblk = -(-tot // bu)
    bend = jnp.cumsum(eblk)
    bstart = bend - eblk
    nblk = (nt * ru) // bu + ne + 1
    blk = jnp.arange(nblk, dtype=jnp.int32)
    e = jnp.sum((bend[None, :] <= blk[:, None]).astype(jnp.int32), axis=1)
    ec = jnp.minimum(e, ne)
    w = (blk - bstart[ec])[:, None] * bu + jnp.arange(bu, dtype=jnp.int32)[None, :]
    is_pad = (e > ne)[:, None] | (w >= tot[ec][:, None])
    so = seg_off[ec]
    sv = (seg_src - seg_off)[ec]
    inside = so[:, None, :] <= w[:, :, None]
    last = inside & ~jnp.concatenate([inside[:, :, 1:], jnp.zeros_like(inside[:, :, :1])], axis=2)
    unit = (jnp.sum(jnp.where(last, sv[:, None, :], 0), axis=2) + w).astype(jnp.int32)
    dump = nt * ru + jnp.arange(bu, dtype=jnp.int32)[None, :]
    tab = jnp.concatenate([jnp.where(is_pad, 0, unit), jnp.where(is_pad, dump, unit)], axis=1)
    pad_l = -(-2 * bu // LANES) * LANES - 2 * bu
    tab = jnp.pad(tab, ((0, 0), (0, pad_l))).reshape(nblk, 1, -1).astype(jnp.int32)
    nact = bend[ne].astype(jnp.int32).reshape(1)
    return tab, e.astype(jnp.int32), nact, nblk


def _moe_ffn_body(bexp_ref, nact_ref, tabp_ref, tab_ref, tabn_ref, xs_hbm, wup_ref, bup_ref, wdn_ref, bdn_ref,
                  out_hbm, xbuf, obuf, sem_in, sem_out, *, bu, ne):
    b = pl.program_id(0)
    nact = nact_ref[0]
    slot = b % 2
    dw = xbuf.shape[-1]

    def start_in(tref, s):
        for j in range(bu):
            pltpu.make_async_copy(xs_hbm.at[tref[0, 0, j]], xbuf.at[s, j], sem_in.at[s]).start()

    def wait_in(s):
        for j in range(bu):
            pltpu.make_async_copy(xs_hbm.at[0], xbuf.at[s, j], sem_in.at[s]).wait()

    def start_out(tref):
        for j in range(bu):
            pltpu.make_async_copy(obuf.at[j], out_hbm.at[tref[0, 0, bu + j]], sem_out).start()

    def wait_out():
        for j in range(bu):
            pltpu.make_async_copy(obuf.at[j], out_hbm.at[0], sem_out).wait()

    def send_and_fetch():
        start_out(tabp_ref)
        start_in(tabn_ref, 1 - slot)

    def ffn():
        x = _unpack_pairs(xbuf[slot].reshape(bu * UNIT, dw)).astype(BF16)
        h = _dot(x, wup_ref[0]) + bup_ref[0]
        f = h.shape[1] // 2
        glu = jnp.minimum(h[:, :f], SWIGLU_LIMIT)
        lin = jnp.clip(h[:, f:], -SWIGLU_LIMIT, SWIGLU_LIMIT)
        act = glu * (1.0 / (1.0 + jnp.exp(-SWIGLU_ALPHA * glu))) * (lin + 1.0)
        out = _dot(act.astype(BF16), wdn_ref[0]) + bdn_ref[0]
        return _pack_pairs(out.astype(BF16).astype(F32)).reshape(bu, UNIT, dw)

    def block(result):
        wait_in(slot)
        send_and_fetch()
        res = result()
        wait_out()
        obuf[...] = res

    def zero_block():
        return jnp.zeros(obuf.shape, jnp.uint32)

    @pl.when(b < nact)
    def _():
        @pl.when(b == 0)
        def _():
            obuf[...] = jnp.zeros(obuf.shape, jnp.uint32)
            start_in(tab_ref, 0)

        e = bexp_ref[b]

        @pl.when(e < ne)
        def _():
            block(ffn)

        @pl.when(e >= ne)
        def _():
            block(zero_block)

        @pl.when(b == nact - 1)
        def _():
            start_out(tab_ref)
            wait_out()
            wait_in(1 - slot)
            nu = out_hbm.shape[0] - bu
            for j in range(bu):
                pltpu.make_async_copy(obuf.at[j], out_hbm.at[nu + j], sem_out).start()
            wait_out()


def _moe_ffn(xs, cp, wup_b, bup, wdn_b, bdn, rt):
    ne, d, f2 = wup_b.shape
    bu = FFN_ROWS // UNIT
    tab, bexp, nact, nblk = _moe_tables(cp, rt, bu)
    nu = xs.shape[0] // UNIT
    dw = xs.shape[1]
    xs3 = xs.reshape(nu, UNIT, dw)
    tl = tab.shape[-1]
    wsel = lambda b, be, na: (jnp.minimum(be[b], ne - 1), 0, 0)
    out = pl.pallas_call(
        functools.partial(_moe_ffn_body, bu=bu, ne=ne),
        grid_spec=pltpu.PrefetchScalarGridSpec(
            num_scalar_prefetch=2,
            grid=(nblk,),
            in_specs=[pl.BlockSpec((1, 1, tl), lambda b, be, na: (jnp.maximum(b - 1, 0), 0, 0),
                                   memory_space=pltpu.SMEM),
                      pl.BlockSpec((1, 1, tl), lambda b, be, na: (b, 0, 0), memory_space=pltpu.SMEM),
                      pl.BlockSpec((1, 1, tl), lambda b, be, na: (jnp.minimum(b + 1, nblk - 1), 0, 0),
                                   memory_space=pltpu.SMEM),
                      pl.BlockSpec(memory_space=pl.ANY),
                      pl.BlockSpec((1, d, f2), wsel), pl.BlockSpec((1, 1, f2), wsel),
                      pl.BlockSpec((1, f2 // 2, d), wsel), pl.BlockSpec((1, 1, d), wsel)],
            out_specs=pl.BlockSpec(memory_space=pl.ANY),
            scratch_shapes=[pltpu.VMEM((2, bu, UNIT, dw), jnp.uint32), pltpu.VMEM((bu, UNIT, dw), jnp.uint32),
                            pltpu.SemaphoreType.DMA((2,)), pltpu.SemaphoreType.DMA(())]),
        out_shape=jax.ShapeDtypeStruct((nu + bu, UNIT, dw), jnp.uint32),
        compiler_params=_cparams(("arbitrary",)),
    )(bexp, nact, tab, tab, tab, xs3, wup_b, bup.reshape(ne, 1, f2), wdn_b, bdn.reshape(ne, 1, d))
    return out.reshape((nu + bu) * UNIT, dw)


def _moe_combine_body(o_ref, pos_ref, info_ref, h_ref, fw_ref, y_ref, *, rt):
    pos = pos_ref[...]
    info = info_ref[...]
    lane = lax.broadcasted_iota(jnp.int32, (pos.shape[0], rt), 1).astype(F32)
    selt = jnp.where(lane == pos[:, 0:1], info[:, TOP_K:TOP_K + 1], 0.0)
    for k in range(1, TOP_K):
        selt = selt + jnp.where(lane == pos[:, k:k + 1], info[:, TOP_K + k:TOP_K + k + 1], 0.0)
    y = _dot(selt.astype(BF16), _unpack_pairs(o_ref[...]).astype(BF16))
    y_ref[...] = _rms(h_ref[...] + y, fw_ref[...])


def _moe_combine(outs, pos, info, h, final_w, rt):
    t, d = h.shape
    tt = min(TOKEN_TILE, t)
    row = lambda i: (i, 0)
    return pl.pallas_call(
        functools.partial(_moe_combine_body, rt=rt),
        grid=(t // tt,),
        in_specs=[pl.BlockSpec((rt, d // 2), row), pl.BlockSpec((tt, 2 * TOP_K), row), pl.BlockSpec((tt, 2 * TOP_K), row),
                  pl.BlockSpec((tt, d), row), pl.BlockSpec((1, d), lambda i: (0, 0))],
        out_specs=pl.BlockSpec((tt, d), row),
        out_shape=jax.ShapeDtypeStruct((t, d), F32),
        compiler_params=_cparams(("parallel",)),
    )(outs, pos, info, h, final_w.reshape(1, d))


def _group(x3, h0_re, h0_im, past, tc, wts):
    n, l, d = x3.shape
    x = x3.reshape(n * l, d)
    sw, aw = wts['sw'], wts['aw']
    hd = aw // (2 * ATT_HEADS)
    prompt = past is None
    proj = _inproj(x, wts['attn_norm_w'], wts['w_in'], sw, aw, hd ** -0.5, pack=prompt, l=l)
    u, k, v, qb, kb = proj[:5]
    heads = (n, l, ATT_HEADS)
    if prompt:
        vt, u2 = proj[5:]
        k_out = k.reshape(n, ATT_HEADS, 2, hd, l).transpose(0, 4, 1, 2, 3)
        v_out = v.reshape(heads + (2 * hd,))
        ys, st_re, st_im = _s5(u2, h0_re, h0_im, wts['s5'][tc], n, l, tc, packed=True)
        o = _attn_prompt(qb, kb, vt, wts['lam'], wts['subln_w'], n, l, wts['post'])
    else:
        k_out = k.reshape(heads + (2, hd))
        v_out = v.reshape(heads + (2 * hd,))
        ys, st_re, st_im = _s5(u, h0_re, h0_im, wts['s5'][tc], n, l, tc, packed=False)
        o = _attn_sample(qb, kb, proj[5], past[0], past[1], past[2], wts['lam'], wts['subln_w'], n, l, wts['post'])
    h, xn, info, infot = _mix(ys, u, wts['d_skip'], o, x, wts['glu_w'], wts['glu_b'], wts['ssm_norm_w'],
                              wts['w_out'], wts['ffn_norm_w'], wts['router_w'], wts['router_b'], packed=prompt)
    ne = wts['w_up'].shape[0]
    rt = _seg_rows(infot.shape[2], ne)
    xs, cpl, pos = _moe_sort(xn, infot, ne)
    outs = _moe_ffn(xs, cpl[:, :, 0], wts['w_up'], wts['b_up'], wts['w_down'], wts['b_down'], rt)
    y = _moe_combine(outs, pos, info, h, wts['final_norm_w'], rt)
    return y.reshape(n, l, d), k_out[None], v_out[None], st_re[None], st_im[None]


def kernel(x_prompt, x_sample, cache_k, cache_v, state_ssm_re, state_ssm_im, page_table, attn_norm_w, w_in, ssm_lambda_re, ssm_lambda_im, ssm_log_dt, ssm_b_re, ssm_b_im, ssm_c_re, ssm_c_im, ssm_d, ssm_glu_w, ssm_glu_b, ssm_norm_w, diff_lambda_q1, diff_lambda_k1, diff_lambda_q2, diff_lambda_k2, subln_w, w_out, ffn_norm_w, router_w, router_b, w_up, b_up, w_down, b_down, final_norm_w):
    assert w_in.shape[0] == 1, "single-layer trunk"
    g = ssm_lambda_re.shape[1]
    sw = g * SSM_GROUP
    aw = (w_in.shape[-1] - sw) // 3
    ne = router_w.shape[-1]
    lam_init = 0.8 - 0.6 * math.exp(-0.3 * 0)
    lam = (jnp.exp(jnp.sum(diff_lambda_q1[0] * diff_lambda_k1[0]))
           - jnp.exp(jnp.sum(diff_lambda_q2[0] * diff_lambda_k2[0])) + lam_init).astype(F32)
    t_dec = x_sample.shape[1]
    s5_args = (ssm_lambda_re[0], ssm_lambda_im[0], ssm_log_dt[0], ssm_b_re[0], ssm_b_im[0],
               ssm_c_re[0], ssm_c_im[0])
    wts = {
        'sw': sw, 'aw': aw, 'lam': lam, 'post': 1.0 - lam_init,
        'attn_norm_w': attn_norm_w[0], 'w_in': w_in[0].astype(BF16),
        's5': {tc: _s5_params(*s5_args, tc) for tc in {S5_CHUNK, t_dec}}, 'd_skip': ssm_d[0],
        'glu_w': ssm_glu_w[0].astype(BF16), 'glu_b': ssm_glu_b[0], 'ssm_norm_w': ssm_norm_w[0],
        'subln_w': subln_w[0], 'w_out': w_out[0].astype(BF16), 'ffn_norm_w': ffn_norm_w[0],
        'router_w': jnp.pad(router_w[0], ((0, 0), (0, LANES - ne))).astype(BF16),
        'router_b': jnp.pad(router_b[0], (0, LANES - ne), constant_values=NEG_BIG).reshape(1, LANES),
        'w_up': _wperm(w_up[0]),
        'b_up': jnp.concatenate([b_up[0, :, 0::2], b_up[0, :, 1::2]], axis=-1),
        'w_down': w_down[0].astype(BF16), 'b_down': b_down[0], 'final_norm_w': final_norm_w,
    }
    nb = x_prompt.shape[0]
    zeros = jnp.zeros((nb, g, SSM_STATE), F32)
    yp, kp, vp, srp, sip = _group(x_prompt, zeros, zeros, None, S5_CHUNK, wts)
    ys, ks, vs, srs, sis = _group(x_sample, state_ssm_re[0], state_ssm_im[0],
                                  (cache_k[0], cache_v[0], page_table), t_dec, wts)
    return (yp, ys, kp, vp, srp, sip, ks, vs, srs, sis)
```

```python
import functools
import math

import jax
import jax.numpy as jnp
from jax import lax
from jax.experimental import pallas as pl
from jax.experimental.pallas import tpu as pltpu

F32 = jnp.float32
BF16 = jnp.bfloat16

SSM_GROUP = 16
SSM_STATE = 64
ATT_HEADS = 4
TOP_K = 4
SWIGLU_LIMIT = 7.0
SWIGLU_ALPHA = 1.702
RMS_EPS = 1e-6
NEG_BIG = -1e30

LANES = 128
SUBLANES = 8
VMEM_LIMIT = 56 * 1024 * 1024

INPROJ_ROWS = 512
ATTN_Q = 256
ATTN_K = 512
S5_CHUNK = 16
S5_ROWS = 512
MIX_TILE = 512
TOKEN_TILE = 256
FFN_ROWS = 512
UNIT = SUBLANES
DEC_PAGES = 8


def _cparams(sem):
    return pltpu.CompilerParams(dimension_semantics=sem, vmem_limit_bytes=VMEM_LIMIT)


def _rms(x, w):
    ms = jnp.mean(x * x, axis=-1, keepdims=True)
    return x * lax.rsqrt(ms + RMS_EPS) * w


def _dot(a, b):
    return jnp.dot(a, b, preferred_element_type=F32)


def _dot_nt(a, b):
    return lax.dot_general(a, b, (((1,), (1,)), ((), ())), preferred_element_type=F32)


def _pack_pairs(x):
    half = x.shape[1] // 2
    lo = lax.bitcast_convert_type(x[:, :half], jnp.uint32)
    hi = lax.bitcast_convert_type(x[:, half:], jnp.uint32)
    return lax.shift_right_logical(lo, jnp.uint32(16)) | (hi & jnp.uint32(0xFFFF0000))


def _unpack_pairs(w):
    lo = lax.bitcast_convert_type(lax.shift_left(w, jnp.uint32(16)), F32)
    hi = lax.bitcast_convert_type(w & jnp.uint32(0xFFFF0000), F32)
    return jnp.concatenate([lo, hi], axis=1)


def _lane_mask(shape, lo, width):
    lane = lax.broadcasted_iota(jnp.int32, shape, 1)
    return (lane >= lo) & (lane < lo + width)


def _inproj_body(x_ref, nw_ref, w_ref, *refs, sw, aw, scale, pack):
    xn = _rms(x_ref[...], nw_ref[...])
    proj = _dot(xn.astype(BF16), w_ref[...])
    q = proj[:, sw:sw + aw]
    k = proj[:, sw + aw:sw + 2 * aw]
    v = proj[:, sw + 2 * aw:]
    if not pack:
        u_ref, k_ref, v_ref, qb_ref, kb_ref, vb_ref = refs
        k_ref[...] = k
        v_ref[...] = v
        vb_ref[...] = v.astype(BF16)
    else:
        u_ref, kt_ref, v4_ref, qb_ref, kb_ref, vt_ref, u2_ref, u_s = refs
        for vt in range(sw // LANES):
            u_s[vt] = proj[:, vt * LANES:(vt + 1) * LANES]
        vtr = v.T
        vt_ref[0] = vtr.astype(BF16)
        kt_ref[0] = k.T
        vd = aw // ATT_HEADS
        for h in range(ATT_HEADS):
            v4_ref[pl.ds(h, v.shape[0], stride=ATT_HEADS), :] = v[:, h * vd:(h + 1) * vd]
    u_ref[...] = proj[:, :sw]
    qb_ref[...] = (q * scale).astype(BF16)
    kb_ref[...] = k.astype(BF16)
    if pack:
        nk = u_s.shape[1] // S5_CHUNK
        per = LANES // SSM_GROUP
        for vt in range(sw // LANES):
            for half in range(S5_CHUNK // per):
                accs = [None] * per
                for s8 in range(per):
                    xv = u_s[vt, pl.ds(half * per + s8, nk, stride=S5_CHUNK), :]
                    mask = _lane_mask(xv.shape, s8 * SSM_GROUP, SSM_GROUP)
                    for gp in range(per):
                        shift = (SSM_GROUP * (s8 - gp)) % LANES
                        r = pltpu.roll(xv, shift, axis=1) if shift else xv
                        accs[gp] = r if s8 == 0 else jnp.where(mask, r, accs[gp])
                for gp in range(per):
                    u2_ref[vt * per + gp, :, half * LANES:(half + 1) * LANES] = accs[gp].astype(BF16)


def _inproj(x, norm_w, w_in_b, sw, aw, scale, pack, l=None):
    t, d = x.shape
    tm = min(INPROJ_ROWS, t)
    row = lambda i: (i, 0)
    fix = lambda i: (0, 0)
    f32o = lambda w: jax.ShapeDtypeStruct((t, w), F32)
    b16o = lambda w: jax.ShapeDtypeStruct((t, w), BF16)
    if pack:
        per = l // tm
        vd = aw // ATT_HEADS
        kv_specs = [pl.BlockSpec((1, aw, tm), lambda i: (i // per, 0, i % per)),
                    pl.BlockSpec((tm * ATT_HEADS, vd), row)]
        kv_shape = [jax.ShapeDtypeStruct((t // l, aw, l), F32), jax.ShapeDtypeStruct((t * ATT_HEADS, vd), F32)]
    else:
        kv_specs = [pl.BlockSpec((tm, aw), row)] * 2
        kv_shape = [f32o(aw), f32o(aw)]
    out_specs = [pl.BlockSpec((tm, sw), row)] + kv_specs + [pl.BlockSpec((tm, aw), row)] * 2
    out_shape = [f32o(sw)] + kv_shape + [b16o(aw), b16o(aw)]
    scratch = []
    if pack:
        g = sw // SSM_GROUP
        kd = S5_CHUNK * SSM_GROUP
        out_specs += [pl.BlockSpec((1, aw, tm), lambda i: (i, 0, 0)),
                      pl.BlockSpec((g, tm // S5_CHUNK, kd), lambda i: (0, i, 0))]
        out_shape += [jax.ShapeDtypeStruct((t // tm, aw, tm), BF16),
                      jax.ShapeDtypeStruct((g, t // S5_CHUNK, kd), BF16)]
        scratch = [pltpu.VMEM((sw // LANES, tm, LANES), F32)]
    else:
        out_specs += [pl.BlockSpec((tm, aw), row)]
        out_shape += [b16o(aw)]
    return pl.pallas_call(
        functools.partial(_inproj_body, sw=sw, aw=aw, scale=scale, pack=pack),
        grid=(t // tm,),
        in_specs=[pl.BlockSpec((tm, d), row), pl.BlockSpec((1, d), fix), pl.BlockSpec(w_in_b.shape, fix)],
        out_specs=out_specs, out_shape=out_shape, scratch_shapes=scratch,
        compiler_params=_cparams(("parallel",)),
    )(x, norm_w.reshape(1, d), w_in_b)


def _s5_params(lam_re, lam_im, log_dt, b_re, b_im, c_re, c_im, tc):
    hp = lax.Precision.HIGHEST
    g, p = lam_re.shape
    c = b_re.shape[-1]
    dt = jnp.exp(log_dt)[:, None]
    mag = jnp.exp(lam_re * dt)
    a_re = mag * jnp.cos(lam_im * dt)
    a_im = mag * jnp.sin(lam_im * dt)
    den = lam_re * lam_re + lam_im * lam_im
    nr = a_re - 1.0
    cf_re = (nr * lam_re + a_im * lam_im) / den
    cf_im = (a_im * lam_re - nr * lam_im) / den
    bb_re = cf_re[..., None] * b_re - cf_im[..., None] * b_im
    bb_im = cf_re[..., None] * b_im + cf_im[..., None] * b_re
    pr, pi = [jnp.ones_like(a_re)], [jnp.zeros_like(a_re)]
    for _ in range(tc):
        pr, pi = pr + [pr[-1] * a_re - pi[-1] * a_im], pi + [pr[-1] * a_im + pi[-1] * a_re]
    pr = jnp.stack(pr)
    pi = jnp.stack(pi)
    ab_re = pr[:tc, :, :, None] * bb_re - pi[:tc, :, :, None] * bb_im
    ab_im = pr[:tc, :, :, None] * bb_im + pi[:tc, :, :, None] * bb_re
    kern = (jnp.einsum('gdp,tgpc->tgdc', c_re, ab_re, precision=hp)
            - jnp.einsum('gdp,tgpc->tgdc', c_im, ab_im, precision=hp))
    lag = jnp.arange(tc)[None, :] - jnp.arange(tc)[:, None]
    kt = kern[jnp.clip(lag, 0, tc - 1)]
    kt = jnp.where((lag >= 0)[:, :, None, None, None], kt, 0.0)
    m = kt.transpose(2, 0, 4, 1, 3).reshape(g, tc * c, tc * c)
    wr = ab_re[::-1].transpose(1, 0, 3, 2).reshape(g, tc * c, p)
    wi = ab_im[::-1].transpose(1, 0, 3, 2).reshape(g, tc * c, p)
    w = jnp.concatenate([wr, wi], axis=-1)
    wsw = jnp.concatenate([wi, wr], axis=-1)
    e_re = c_re[None] * pr[1:, :, None, :] - c_im[None] * pi[1:, :, None, :]
    e_im = c_re[None] * pi[1:, :, None, :] + c_im[None] * pr[1:, :, None, :]
    v = jnp.concatenate([e_re.transpose(1, 3, 0, 2).reshape(g, p, tc * c),
                         -e_im.transpose(1, 3, 0, 2).reshape(g, p, tc * c)], axis=1)
    ar = jnp.concatenate([pr[tc], pr[tc]], axis=-1)[:, None, :]
    ai = jnp.concatenate([-pi[tc], pi[tc]], axis=-1)[:, None, :]
    return m.astype(BF16), w.astype(BF16), wsw.astype(BF16), v.astype(BF16), ar, ai


def _s5_body(u_ref, m_ref, w_ref, wsw_ref, v_ref, ar_ref, ai_ref, h0_ref, h0s_ref,
             y_ref, hout_ref, h_s, hs_s, s_s, ssw_s, hin_s, *, cb, nb, seq_major):
    @pl.when(pl.program_id(1) == 0)
    def _():
        h_s[...] = h0_ref[0]
        hs_s[...] = h0s_ref[0]

    kd = u_ref.shape[-1]
    ub = u_ref[0].reshape(cb * nb, kd).astype(BF16)
    yi = _dot(ub, m_ref[0])
    s_s[...] = _dot(ub, w_ref[0])
    ssw_s[...] = _dot(ub, wsw_ref[0])
    ar = ar_ref[0]
    ai = ai_ref[0]
    h = h_s[...]
    hs = hs_s[...]
    for c in range(cb):
        idx = pl.ds(c, nb, stride=cb) if (seq_major and cb > 1) else pl.ds(c * nb, nb)
        hin_s[idx, :] = h
        h, hs = ar * h + ai * hs + s_s[idx, :], ar * hs - ai * h + ssw_s[idx, :]
    h_s[...] = h
    hs_s[...] = hs
    hout_ref[0] = h
    yo = _dot(hin_s[...].astype(BF16), v_ref[0])
    y_ref[0] = (yi + yo).reshape(y_ref.shape[1:])


def _s5(u_in, h0_re, h0_im, prm, n, l, tc, packed):
    m, w, wsw, v, ar, ai = prm
    g = m.shape[0]
    c = SSM_GROUP
    p2 = 2 * SSM_STATE
    nc = l // tc
    kd = tc * c
    cb = max(1, min(nc, S5_ROWS // n))
    assert nc % cb == 0 and n % SUBLANES == 0
    if packed:
        u2 = u_in.reshape(g, n, nc, kd)
        blk_shape, blk = (1, n, cb, kd), (lambda gi, j: (gi, 0, j, 0))
    else:
        u2 = u_in.reshape(n, nc, tc, g, c).transpose(3, 1, 0, 2, 4).reshape(g, nc, n, kd).astype(BF16)
        blk_shape, blk = (1, cb, n, kd), (lambda gi, j: (gi, j, 0, 0))
    h0 = jnp.concatenate([h0_re, h0_im], axis=-1).transpose(1, 0, 2)
    h0s = jnp.concatenate([h0_im, h0_re], axis=-1).transpose(1, 0, 2)
    par = lambda gi, j: (gi, 0, 0)
    y2, hout = pl.pallas_call(
        functools.partial(_s5_body, cb=cb, nb=n, seq_major=packed),
        grid=(g, nc // cb),
        in_specs=[pl.BlockSpec(blk_shape, blk),
                  pl.BlockSpec((1, kd, kd), par), pl.BlockSpec((1, kd, p2), par), pl.BlockSpec((1, kd, p2), par),
                  pl.BlockSpec((1, p2, kd), par),
                  pl.BlockSpec((1, 1, p2), par), pl.BlockSpec((1, 1, p2), par),
                  pl.BlockSpec((1, n, p2), par), pl.BlockSpec((1, n, p2), par)],
        out_specs=[pl.BlockSpec(blk_shape, blk), pl.BlockSpec((1, n, p2), par)],
        out_shape=[jax.ShapeDtypeStruct(u2.shape, F32), jax.ShapeDtypeStruct((g, n, p2), F32)],
        scratch_shapes=[pltpu.VMEM((n, p2), F32), pltpu.VMEM((n, p2), F32), pltpu.VMEM((cb * n, p2), F32),
                        pltpu.VMEM((cb * n, p2), F32), pltpu.VMEM((cb * n, p2), F32)],
        compiler_params=_cparams(("parallel", "arbitrary")),
    )(u2, m, w, wsw, v, ar, ai, h0, h0s)
    if packed:
        y = y2.reshape(g, n * nc, kd)
    else:
        y = y2.reshape(g, nc, n, tc, c).transpose(2, 1, 3, 0, 4).reshape(n * l, g * c)
    hout = hout.transpose(1, 0, 2)
    return y, hout[..., :SSM_STATE], hout[..., SSM_STATE:]


def _attn_body(lam_ref, q_ref, k_ref, vt_ref, swb_ref, o_ref, q2_s, m_s, acc_s, *, tq, tk, vd, post):
    qi = pl.program_id(2)
    hd = vd // 2
    q = q_ref[0]
    lane = lax.broadcasted_iota(jnp.int32, q.shape, 1)
    zero = jnp.zeros_like(q)
    q2_s[0:tq, :] = jnp.where(lane < hd, q, zero)
    q2_s[tq:, :] = jnp.where(lane >= hd, q, zero)
    m_s[...] = jnp.full(m_s.shape, NEG_BIG, F32)
    acc_s[...] = jnp.zeros(acc_s.shape, F32)
    ones = jnp.ones((vd, tk), BF16)

    def scores(j):
        start = pl.multiple_of(j * tk, tk)
        return _dot_nt(k_ref[0, pl.ds(start, tk), :], q2_s[...])

    def step(j, st2, off):
        vaug = jnp.concatenate([vt_ref[j], ones], axis=0)
        for c in range(2):
            cs = slice(c * tq, (c + 1) * tq)
            st = st2[:, cs]
            if off is not None:
                key = lax.broadcasted_iota(jnp.int32, st.shape, 0)
                row = lax.broadcasted_iota(jnp.int32, st.shape, 1)
                st = jnp.where(key <= row + off, st, NEG_BIG)
            m_old = m_s[:, cs]
            m_new = jnp.maximum(m_old, jnp.max(st, axis=0, keepdims=True))
            alpha = jnp.exp(m_old - m_new)
            p = jnp.exp(st - m_new).astype(BF16)
            acc_s[:, cs] = alpha * acc_s[:, cs] + _dot(vaug, p)
            m_s[:, cs] = m_new

    def full_step(j, st2):
        nxt = scores(j + 1)
        step(j, st2, None)
        return nxt

    nfull = (qi * tq) // tk
    st2 = lax.fori_loop(0, nfull, full_step, scores(0))
    step(nfull, st2, qi * tq - nfull * tk)
    acc = acc_s[...]
    ot = (acc[:vd, :tq] / acc[vd:vd + 1, :tq]) - lam_ref[0] * (acc[:vd, tq:] / acc[vd:vd + 1, tq:])
    ms = jnp.mean(ot * ot, axis=0, keepdims=True)
    ot = ot * lax.rsqrt(ms + RMS_EPS) * swb_ref[...] * post
    o_ref[0] = ot.T.astype(o_ref.dtype)


def _attn_prompt(qb, kb, vt, lam, subln_w, n, l, post):
    aw = qb.shape[-1]
    vd = aw // ATT_HEADS
    tk = vt.shape[-1]
    tq = min(ATTN_Q, l)
    assert l % tk == 0 and tk % tq == 0
    q3, k3 = qb.reshape(n, l, aw), kb.reshape(n, l, aw)
    swb = jnp.broadcast_to(subln_w.reshape(vd, 1), (vd, tq))
    return pl.pallas_call(
        functools.partial(_attn_body, tq=tq, tk=tk, vd=vd, post=post),
        grid=(n, ATT_HEADS, l // tq),
        in_specs=[pl.BlockSpec(memory_space=pltpu.SMEM),
                  pl.BlockSpec((1, tq, vd), lambda b, h, i: (b, i, h)),
                  pl.BlockSpec((1, l, vd), lambda b, h, i: (b, 0, h)),
                  pl.BlockSpec((l // tk, vd, tk), lambda b, h, i: (b, h, 0)),
                  pl.BlockSpec((vd, tq), lambda b, h, i: (0, 0))],
        out_specs=pl.BlockSpec((1, tq, vd), lambda b, h, i: (b, i, h)),
        out_shape=jax.ShapeDtypeStruct((n, l, aw), BF16),
        scratch_shapes=[pltpu.VMEM((2 * tq, vd), BF16), pltpu.VMEM((1, 2 * tq), F32),
                        pltpu.VMEM((2 * vd, 2 * tq), F32)],
        compiler_params=_cparams(("parallel", "parallel", "arbitrary")),
    )(lam.reshape(1), q3, k3, vt, swb).reshape(n * l, aw)


def _attn_dec_body(pt_ref, lam_ref, qr_ref, kn_ref, vn_ref, sw_ref, *rest, pg, t_new, vd, post):
    k_refs, v_refs = rest[:pg], rest[pg:2 * pg]
    o_ref, m_s, l_s, acc_s = rest[2 * pg:]
    j = pl.program_id(1)

    @pl.when(j == 0)
    def _():
        m_s[...] = jnp.full(m_s.shape, NEG_BIG, F32)
        l_s[...] = jnp.zeros(l_s.shape, F32)
        acc_s[...] = jnp.zeros(acc_s.shape, F32)

    qr = qr_ref[0]
    rows = 2 * t_new
    page = k_refs[0].shape[2]

    def update(s, vhs):
        m_old = m_s[...]
        m_new = jnp.maximum(m_old, jnp.max(s, axis=-1, keepdims=True))
        alpha = jnp.exp(m_old - m_new)
        p = jnp.exp(s - m_new).astype(BF16)
        l_s[...] = alpha * l_s[...] + jnp.sum(p.astype(F32), axis=-1, keepdims=True)
        for h in range(ATT_HEADS):
            hs = slice(h * rows, (h + 1) * rows)
            pv = _dot(p[hs, :page], vhs[h][0])
            for i in range(1, len(vhs[h])):
                pv = pv + _dot(p[hs, i * page:(i + 1) * page], vhs[h][i])
            acc_s[hs, :] = alpha[hs] * acc_s[hs, :] + pv
        m_s[...] = m_new

    s = jnp.concatenate([_dot(qr, kr[0].astype(BF16)) for kr in k_refs], axis=1)
    update(s, [[vr[0, pl.ds(h, page, stride=ATT_HEADS), :].astype(BF16) for vr in v_refs]
               for h in range(ATT_HEADS)])

    @pl.when(j == pl.num_programs(1) - 1)
    def _():
        sn = _dot_nt(qr, kn_ref[0])
        r = lax.broadcasted_iota(jnp.int32, sn.shape, 0) % t_new
        c = lax.broadcasted_iota(jnp.int32, sn.shape, 1)
        vn = vn_ref[0]
        update(jnp.where(c <= r, sn, NEG_BIG), [[vn[:, h * vd:(h + 1) * vd]] for h in range(ATT_HEADS)])
        acc = acc_s[...]
        l = l_s[...]
        for h in range(ATT_HEADS):
            blk = acc[h * rows:(h + 1) * rows] / l[h * rows:(h + 1) * rows]
            o = blk[:t_new] - lam_ref[0] * blk[t_new:]
            o_ref[0, :, h * vd:(h + 1) * vd] = _rms(o, sw_ref[...]) * post


def _attn_sample(qb, kb, vb, cache_k, cache_v, page_table, lam, subln_w, nseq, t_new, post):
    aw = qb.shape[-1]
    vd = aw // ATT_HEADS
    hd = vd // 2
    n_pool, page = cache_k.shape[0], cache_k.shape[1]
    n_pages = page_table.shape[1]
    pg = math.gcd(DEC_PAGES, n_pages)
    ck = cache_k.reshape(n_pool, page, aw).transpose(0, 2, 1)
    cv = cache_v.reshape(n_pool, page * ATT_HEADS, vd)
    q5 = qb.reshape(nseq, t_new, ATT_HEADS, 2, hd)
    qr = jnp.einsum('bqhcd,hx,cy->bhcqxyd', q5, jnp.eye(ATT_HEADS, dtype=BF16), jnp.eye(2, dtype=BF16))
    nr = ATT_HEADS * 2 * t_new
    qr = qr.reshape(nseq, nr, aw)
    pad = ((0, 0), (0, page - t_new), (0, 0))
    kn = jnp.pad(kb.reshape(nseq, t_new, aw), pad)
    vn = jnp.pad(vb.reshape(nseq, t_new, aw), pad)
    seq = lambda b, j, pt: (b, 0, 0)
    pick = lambda b, j, pt, i: (pt[b, j * pg + i], 0, 0)
    k_specs = [pl.BlockSpec((1, aw, page), functools.partial(pick, i=i)) for i in range(pg)]
    v_specs = [pl.BlockSpec((1, page * ATT_HEADS, vd), functools.partial(pick, i=i)) for i in range(pg)]
    return pl.pallas_call(
        functools.partial(_attn_dec_body, pg=pg, t_new=t_new, vd=vd, post=post),
        grid_spec=pltpu.PrefetchScalarGridSpec(
            num_scalar_prefetch=1,
            grid=(nseq, n_pages // pg),
            in_specs=[pl.BlockSpec(memory_space=pltpu.SMEM),
                      pl.BlockSpec((1, nr, aw), seq), pl.BlockSpec((1, page, aw), seq),
                      pl.BlockSpec((1, page, aw), seq), pl.BlockSpec((1, vd), lambda b, j, pt: (0, 0))]
            + k_specs + v_specs,
            out_specs=pl.BlockSpec((1, t_new, aw), seq),
            scratch_shapes=[pltpu.VMEM((nr, 1), F32), pltpu.VMEM((nr, 1), F32), pltpu.VMEM((nr, vd), F32)]),
        out_shape=jax.ShapeDtypeStruct((nseq, t_new, aw), F32),
        compiler_params=_cparams(("parallel", "arbitrary")),
    )(page_table, lam.reshape(1), qr, kn, vn, subln_w.reshape(1, vd), *([ck] * pg), *([cv] * pg)
      ).reshape(nseq * t_new, aw)


def _mix_body(ys_ref, u_ref, dsk_ref, o_ref, x_ref, gw_ref, gb_ref, snw_ref, wo_ref, fnw_ref, rw_ref, rb_ref,
              h_ref, xn_ref, info_ref, infot_ref, *scratch, sw, packed):
    if packed:
        ys_s, = scratch
        nk = ys_s.shape[1] // S5_CHUNK
        per = LANES // SSM_GROUP
        for vt in range(sw // LANES):
            for half in range(S5_CHUNK // per):
                srcs = [ys_ref[vt * per + gp, :, half * LANES:(half + 1) * LANES] for gp in range(per)]
                masks = [_lane_mask(srcs[0].shape, gp * SSM_GROUP, SSM_GROUP) for gp in range(per)]
                for s8 in range(per):
                    val = None
                    for gp in range(per):
                        shift = (SSM_GROUP * (gp - s8)) % LANES
                        r = pltpu.roll(srcs[gp], shift, axis=1) if shift else srcs[gp]
                        val = r if gp == 0 else jnp.where(masks[gp], r, val)
                    ys_s[vt, pl.ds(half * per + s8, nk, stride=S5_CHUNK), :] = val
        ys = jnp.concatenate([ys_s[vt] for vt in range(sw // LANES)], axis=1)
    else:
        ys = ys_ref[...]
    y = jax.nn.gelu(ys + dsk_ref[...] * u_ref[...])
    z = _dot(y.astype(BF16), gw_ref[...]) + gb_ref[...]
    y = y * (1.0 / (1.0 + jnp.exp(-z)))
    yn = _rms(y, snw_ref[...])
    mixed = _dot(yn.astype(BF16), wo_ref[:sw, :]) + _dot(o_ref[...].astype(BF16), wo_ref[sw:, :])
    h = x_ref[...] + mixed
    h_ref[...] = h
    xnb = _rms(h, fnw_ref[...]).astype(BF16)
    xn_ref[...] = xnb
    logits = _dot(xnb, rw_ref[...]) + rb_ref[...]
    lane = lax.broadcasted_iota(jnp.int32, logits.shape, 1).astype(F32)
    work = logits
    vals, idxs = [], []
    for _ in range(TOP_K):
        mx = jnp.max(work, axis=-1, keepdims=True)
        ix = jnp.min(jnp.where(work == mx, lane, float(LANES)), axis=-1, keepdims=True)
        vals.append(mx)
        idxs.append(ix)
        work = jnp.where(lane == ix, -3e38, work)
    es = [jnp.exp(v - vals[0]) for v in vals]
    den = es[0] + es[1] + es[2] + es[3]
    info = jnp.zeros(logits.shape, F32)
    for k in range(TOP_K):
        info = info + jnp.where(lane == float(k), idxs[k], 0.0) + jnp.where(lane == float(TOP_K + k), es[k] / den, 0.0)
    info_ref[...] = info[:, :2 * TOP_K]
    tt = infot_ref.shape[2]
    for s in range(infot_ref.shape[0]):
        infot_ref[s] = info[s * tt:(s + 1) * tt].T[:2 * TOP_K, :]


def _mix(ys, u, d_skip, o, x, glu_w_b, glu_b, ssm_norm_w, w_out_b, ffn_norm_w, router_w_b, router_b_p, packed):
    t, d = x.shape
    sw = u.shape[1]
    aw = o.shape[1]
    tm = min(MIX_TILE, t)
    tt = min(TOKEN_TILE, t)
    nt = t // tm
    row = lambda i: (i, 0)
    fix = lambda i: (0, 0)
    if packed:
        ys_spec = pl.BlockSpec((ys.shape[0], tm // S5_CHUNK, ys.shape[2]), lambda i: (0, i, 0))
        scratch = [pltpu.VMEM((sw // LANES, tm, LANES), F32)]
    else:
        ys_spec = pl.BlockSpec((tm, sw), row)
        scratch = []
    return pl.pallas_call(
        functools.partial(_mix_body, sw=sw, packed=packed),
        grid=(nt,),
        in_specs=[ys_spec, pl.BlockSpec((tm, sw), row), pl.BlockSpec((1, sw), fix),
                  pl.BlockSpec((tm, aw), row), pl.BlockSpec((tm, d), row),
                  pl.BlockSpec((sw, sw), fix), pl.BlockSpec((1, sw), fix), pl.BlockSpec((1, sw), fix),
                  pl.BlockSpec((d, d), fix), pl.BlockSpec((1, d), fix),
                  pl.BlockSpec((d, LANES), fix), pl.BlockSpec((1, LANES), fix)],
        out_specs=[pl.BlockSpec((tm, d), row), pl.BlockSpec((tm, d), row),
                   pl.BlockSpec((tm, 2 * TOP_K), row),
                   pl.BlockSpec((tm // tt, 2 * TOP_K, tt), lambda i: (i, 0, 0))],
        out_shape=[jax.ShapeDtypeStruct((t, d), F32), jax.ShapeDtypeStruct((t, d), BF16),
                   jax.ShapeDtypeStruct((t, 2 * TOP_K), F32), jax.ShapeDtypeStruct((t // tt, 2 * TOP_K, tt), F32)],
        scratch_shapes=scratch,
        compiler_params=_cparams(("parallel",)),
    )(ys, u, d_skip.reshape(1, sw), o, x, glu_w_b, glu_b.reshape(1, sw), ssm_norm_w.reshape(1, sw), w_out_b,
      ffn_norm_w.reshape(1, d), router_w_b, router_b_p)


def _wperm_body(w_ref, p_ref, o_ref):
    f = o_ref.shape[-1] // 2
    grp = 2 * LANES
    for k in range(w_ref.shape[-1] // grp):
        r = _dot(w_ref[0, :, k * grp:(k + 1) * grp].astype(BF16), p_ref[...]).astype(BF16)
        o_ref[0, :, k * LANES:(k + 1) * LANES] = r[:, :LANES]
        o_ref[0, :, f + k * LANES:f + (k + 1) * LANES] = r[:, LANES:]


def _wperm(w_up):
    ne, d, f2 = w_up.shape
    grp = 2 * LANES
    src = jnp.arange(grp)[:, None]
    dst = jnp.arange(grp)[None, :]
    perm = (dst == (src % 2) * LANES + src // 2).astype(BF16)
    return pl.pallas_call(
        _wperm_body,
        grid=(ne,),
        in_specs=[pl.BlockSpec((1, d, f2), lambda e: (e, 0, 0)), pl.BlockSpec((grp, grp), lambda e: (0, 0))],
        out_specs=pl.BlockSpec((1, d, f2), lambda e: (e, 0, 0)),
        out_shape=jax.ShapeDtypeStruct((ne, d, f2), BF16),
        compiler_params=_cparams(("parallel",)),
    )(w_up, perm)


def _seg_rows(tt, ne):
    worst = tt * TOP_K + ne * (UNIT - 1)
    return -(-worst // LANES) * LANES


def _moe_sort_body(x_ref, infot_ref, xs_ref, cp_ref, pos_ref, *, tt, ne, rt):
    infot = infot_ref[0]
    sub = lax.broadcasted_iota(jnp.int32, (ne, tt), 0).astype(F32)
    onehots = [(sub == infot[k:k + 1, :]).astype(F32) for k in range(TOP_K)]
    et = onehots[0] + onehots[1] + onehots[2] + onehots[3]
    cnt = jnp.sum(et, axis=1, keepdims=True)
    cp = jnp.floor((cnt + (UNIT - 1.0)) * (1.0 / UNIT))
    er = lax.broadcasted_iota(jnp.int32, (ne, ne), 0)
    ec = lax.broadcasted_iota(jnp.int32, (ne, ne), 1)
    cpb = jnp.broadcast_to(cp, (ne, LANES))
    lo = _dot((ec < er).astype(BF16), cpb.astype(BF16))[:, :1] * float(UNIT)
    tr = lax.broadcasted_iota(jnp.int32, (tt, tt), 0)
    tc = lax.broadcasted_iota(jnp.int32, (tt, tt), 1)
    rank = _dot(et.astype(BF16), (tr < tc).astype(BF16))
    base = lo + rank
    poss = [jnp.sum(oh * base, axis=0, keepdims=True) for oh in onehots]
    rows = lax.broadcasted_iota(jnp.int32, (rt, tt), 0).astype(F32)
    sel = (rows == poss[0]).astype(F32)
    for k in range(1, TOP_K):
        sel = sel + (rows == poss[k]).astype(F32)
    xs_ref[...] = _pack_pairs(_dot(sel.astype(BF16), x_ref[...]))
    cp_ref[0] = cpb.astype(jnp.int32)
    pt = jnp.concatenate(poss + [jnp.zeros((LANES - TOP_K, tt), F32)], axis=0)
    pos_ref[...] = pt.T[:, :2 * TOP_K]


def _moe_sort(xn, infot, ne):
    t, d = xn.shape
    nt, _, tt = infot.shape
    rt = _seg_rows(tt, ne)
    return pl.pallas_call(
        functools.partial(_moe_sort_body, tt=tt, ne=ne, rt=rt),
        grid=(nt,),
        in_specs=[pl.BlockSpec((tt, d), lambda i: (i, 0)), pl.BlockSpec((1, 2 * TOP_K, tt), lambda i: (i, 0, 0))],
        out_specs=[pl.BlockSpec((rt, d // 2), lambda i: (i, 0)), pl.BlockSpec((1, ne, LANES), lambda i: (i, 0, 0)),
                   pl.BlockSpec((tt, 2 * TOP_K), lambda i: (i, 0))],
        out_shape=[jax.ShapeDtypeStruct((nt * rt, d // 2), jnp.uint32),
                   jax.ShapeDtypeStruct((nt, ne, LANES), jnp.int32),
                   jax.ShapeDtypeStruct((t, 2 * TOP_K), F32)],
        compiler_params=_cparams(("parallel",)),
    )(xn, infot)


def _moe_tables(cp, rt, bu):
    nt, ne = cp.shape
    ru = rt // UNIT
    lo = jnp.cumsum(cp, axis=1) - cp
    used = jnp.sum(cp, axis=1)
    seg_len = jnp.concatenate([cp.T, (ru - used)[None, :]], axis=0)
    seg_src = jnp.concatenate([(jnp.arange(nt)[:, None] * ru + lo).T,
                               (jnp.arange(nt) * ru + used)[None, :]], axis=0)
    seg_off = jnp.cumsum(seg_len, axis=1) - seg_len
    tot = jnp.sum(seg_len, axis=1)
    eblk = -(-tot // bu)
    bend = jnp.cumsum(eblk)
    bstart = bend - eblk
    nblk = (nt * ru) // bu + ne + 1
    blk = jnp.arange(nblk, dtype=jnp.int32)
    e = jnp.sum((bend[None, :] <= blk[:, None]).astype(jnp.int32), axis=1)
    ec = jnp.minimum(e, ne)
    w = (blk - bstart[ec])[:, None] * bu + jnp.arange(bu, dtype=jnp.int32)[None, :]
    is_pad = (e > ne)[:, None] | (w >= tot[ec][:, None])
    so = seg_off[ec]
    sv = (seg_src - seg_off)[ec]
    inside = so[:, None, :] <= w[:, :, None]
    last = inside & ~jnp.concatenate([inside[:, :, 1:], jnp.zeros_like(inside[:, :, :1])], axis=2)
    unit = (jnp.sum(jnp.where(last, sv[:, None, :], 0), axis=2) + w).astype(jnp.int32)
    dump = nt * ru + jnp.arange(bu, dtype=jnp.int32)[None, :]
    tab = jnp.concatenate([jnp.where(is_pad, 0, unit), jnp.where(is_pad, dump, unit)], axis=1)
    pad_l = -(-2 * bu // LANES) * LANES - 2 * bu
    tab = jnp.pad(tab, ((0, 0), (0, pad_l))).reshape(nblk, 1, -1).astype(jnp.int32)
    nact = bend[ne].astype(jnp.int32).reshape(1)
    return tab, e.astype(jnp.int32), nact, nblk


def _moe_ffn_body(bexp_ref, nact_ref, tab_ref, tabn_ref, xs_hbm, wup_ref, bup_ref, wdn_ref, bdn_ref,
                  out_hbm, xbuf, obuf, sem_in, sem_out, *, bu, ne):
    b = pl.program_id(0)
    nact = nact_ref[0]
    slot = b % 2
    dw = xbuf.shape[-1]

    def start_in(tref, s):
        for j in range(bu):
            pltpu.make_async_copy(xs_hbm.at[tref[0, 0, j]], xbuf.at[s, j], sem_in.at[s]).start()

    def wait_in(s):
        for j in range(bu):
            pltpu.make_async_copy(xs_hbm.at[0], xbuf.at[s, j], sem_in.at[s]).wait()

    def start_out():
        for j in range(bu):
            pltpu.make_async_copy(obuf.at[j], out_hbm.at[tab_ref[0, 0, bu + j]], sem_out).start()

    def wait_out():
        for j in range(bu):
            pltpu.make_async_copy(obuf.at[j], out_hbm.at[0], sem_out).wait()

    @pl.when(b < nact)
    def _():
        @pl.when(b == 0)
        def _():
            start_in(tab_ref, 0)

        wait_in(slot)

        @pl.when(b + 1 < nact)
        def _():
            start_in(tabn_ref, 1 - slot)

        e = bexp_ref[b]

        @pl.when(e < ne)
        def _():
            x = _unpack_pairs(xbuf[slot].reshape(bu * UNIT, dw)).astype(BF16)
            h = _dot(x, wup_ref[0]) + bup_ref[0]
            f = h.shape[1] // 2
            glu = jnp.minimum(h[:, :f], SWIGLU_LIMIT)
            lin = jnp.clip(h[:, f:], -SWIGLU_LIMIT, SWIGLU_LIMIT)
            act = glu * (1.0 / (1.0 + jnp.exp(-SWIGLU_ALPHA * glu))) * (lin + 1.0)
            out = _dot(act.astype(BF16), wdn_ref[0]) + bdn_ref[0]

            @pl.when(b > 0)
            def _():
                wait_out()

            obuf[...] = _pack_pairs(out.astype(BF16).astype(F32)).reshape(bu, UNIT, dw)

        @pl.when(e >= ne)
        def _():
            @pl.when(b > 0)
            def _():
                wait_out()

            obuf[...] = jnp.zeros(obuf.shape, jnp.uint32)

        start_out()

        @pl.when(b == nact - 1)
        def _():
            wait_out()
            nu = out_hbm.shape[0] - bu
            for j in range(bu):
                pltpu.make_async_copy(obuf.at[j], out_hbm.at[nu + j], sem_out).start()
            wait_out()


def _moe_ffn(xs, cp, wup_b, bup, wdn_b, bdn, rt):
    ne, d, f2 = wup_b.shape
    bu = FFN_ROWS // UNIT
    tab, bexp, nact, nblk = _moe_tables(cp, rt, bu)
    nu = xs.shape[0] // UNIT
    dw = xs.shape[1]
    xs3 = xs.reshape(nu, UNIT, dw)
    tl = tab.shape[-1]
    wsel = lambda b, be, na: (jnp.minimum(be[b], ne - 1), 0, 0)
    out = pl.pallas_call(
        functools.partial(_moe_ffn_body, bu=bu, ne=ne),
        grid_spec=pltpu.PrefetchScalarGridSpec(
            num_scalar_prefetch=2,
            grid=(nblk,),
            in_specs=[pl.BlockSpec((1, 1, tl), lambda b, be, na: (b, 0, 0), memory_space=pltpu.SMEM),
                      pl.BlockSpec((1, 1, tl), lambda b, be, na: (jnp.minimum(b + 1, nblk - 1), 0, 0),
                                   memory_space=pltpu.SMEM),
                      pl.BlockSpec(memory_space=pl.ANY),
                      pl.BlockSpec((1, d, f2), wsel), pl.BlockSpec((1, 1, f2), wsel),
                      pl.BlockSpec((1, f2 // 2, d), wsel), pl.BlockSpec((1, 1, d), wsel)],
            out_specs=pl.BlockSpec(memory_space=pl.ANY),
            scratch_shapes=[pltpu.VMEM((2, bu, UNIT, dw), jnp.uint32), pltpu.VMEM((bu, UNIT, dw), jnp.uint32),
                            pltpu.SemaphoreType.DMA((2,)), pltpu.SemaphoreType.DMA(())]),
        out_shape=jax.ShapeDtypeStruct((nu + bu, UNIT, dw), jnp.uint32),
        compiler_params=_cparams(("arbitrary",)),
    )(bexp, nact, tab, tab, xs3, wup_b, bup.reshape(ne, 1, f2), wdn_b, bdn.reshape(ne, 1, d))
    return out.reshape((nu + bu) * UNIT, dw)


def _moe_combine_body(o_ref, pos_ref, info_ref, h_ref, fw_ref, y_ref, *, rt):
    pos = pos_ref[...]
    info = info_ref[...]
    lane = lax.broadcasted_iota(jnp.int32, (pos.shape[0], rt), 1).astype(F32)
    selt = jnp.where(lane == pos[:, 0:1], info[:, TOP_K:TOP_K + 1], 0.0)
    for k in range(1, TOP_K):
        selt = selt + jnp.where(lane == pos[:, k:k + 1], info[:, TOP_K + k:TOP_K + k + 1], 0.0)
    y = _dot(selt.astype(BF16), _unpack_pairs(o_ref[...]).astype(BF16))
    y_ref[...] = _rms(h_ref[...] + y, fw_ref[...])


def _moe_combine(outs, pos, info, h, final_w, rt):
    t, d = h.shape
    tt = min(TOKEN_TILE, t)
    row = lambda i: (i, 0)
    return pl.pallas_call(
        functools.partial(_moe_combine_body, rt=rt),
        grid=(t // tt,),
        in_specs=[pl.BlockSpec((rt, d // 2), row), pl.BlockSpec((tt, 2 * TOP_K), row), pl.BlockSpec((tt, 2 * TOP_K), row),
                  pl.BlockSpec((tt, d), row), pl.BlockSpec((1, d), lambda i: (0, 0))],
        out_specs=pl.BlockSpec((tt, d), row),
        out_shape=jax.ShapeDtypeStruct((t, d), F32),
        compiler_params=_cparams(("parallel",)),
    )(outs, pos, info, h, final_w.reshape(1, d))


def _group(x3, h0_re, h0_im, past, tc, wts):
    n, l, d = x3.shape
    x = x3.reshape(n * l, d)
    sw, aw = wts['sw'], wts['aw']
    hd = aw // (2 * ATT_HEADS)
    prompt = past is None
    proj = _inproj(x, wts['attn_norm_w'], wts['w_in'], sw, aw, hd ** -0.5, pack=prompt, l=l)
    u, k, v, qb, kb = proj[:5]
    heads = (n, l, ATT_HEADS)
    if prompt:
        vt, u2 = proj[5:]
        k_out = k.reshape(n, ATT_HEADS, 2, hd, l).transpose(0, 4, 1, 2, 3)
        v_out = v.reshape(heads + (2 * hd,))
        ys, st_re, st_im = _s5(u2, h0_re, h0_im, wts['s5'][tc], n, l, tc, packed=True)
        o = _attn_prompt(qb, kb, vt, wts['lam'], wts['subln_w'], n, l, wts['post'])
    else:
        k_out = k.reshape(heads + (2, hd))
        v_out = v.reshape(heads + (2 * hd,))
        ys, st_re, st_im = _s5(u, h0_re, h0_im, wts['s5'][tc], n, l, tc, packed=False)
        o = _attn_sample(qb, kb, proj[5], past[0], past[1], past[2], wts['lam'], wts['subln_w'], n, l, wts['post'])
    h, xn, info, infot = _mix(ys, u, wts['d_skip'], o, x, wts['glu_w'], wts['glu_b'], wts['ssm_norm_w'],
                              wts['w_out'], wts['ffn_norm_w'], wts['router_w'], wts['router_b'], packed=prompt)
    ne = wts['w_up'].shape[0]
    rt = _seg_rows(infot.shape[2], ne)
    xs, cpl, pos = _moe_sort(xn, infot, ne)
    outs = _moe_ffn(xs, cpl[:, :, 0], wts['w_up'], wts['b_up'], wts['w_down'], wts['b_down'], rt)
    y = _moe_combine(outs, pos, info, h, wts['final_norm_w'], rt)
    return y.reshape(n, l, d), k_out[None], v_out[None], st_re[None], st_im[None]


def kernel(x_prompt, x_sample, cache_k, cache_v, state_ssm_re, state_ssm_im, page_table, attn_norm_w, w_in, ssm_lambda_re, ssm_lambda_im, ssm_log_dt, ssm_b_re, ssm_b_im, ssm_c_re, ssm_c_im, ssm_d, ssm_glu_w, ssm_glu_b, ssm_norm_w, diff_lambda_q1, diff_lambda_k1, diff_lambda_q2, diff_lambda_k2, subln_w, w_out, ffn_norm_w, router_w, router_b, w_up, b_up, w_down, b_down, final_norm_w):
    assert w_in.shape[0] == 1, "single-layer trunk"
    g = ssm_lambda_re.shape[1]
    sw = g * SSM_GROUP
    aw = (w_in.shape[-1] - sw) // 3
    ne = router_w.shape[-1]
    lam_init = 0.8 - 0.6 * math.exp(-0.3 * 0)
    lam = (jnp.exp(jnp.sum(diff_lambda_q1[0] * diff_lambda_k1[0]))
           - jnp.exp(jnp.sum(diff_lambda_q2[0] * diff_lambda_k2[0])) + lam_init).astype(F32)
    t_dec = x_sample.shape[1]
    s5_args = (ssm_lambda_re[0], ssm_lambda_im[0], ssm_log_dt[0], ssm_b_re[0], ssm_b_im[0],
               ssm_c_re[0], ssm_c_im[0])
    wts = {
        'sw': sw, 'aw': aw, 'lam': lam, 'post': 1.0 - lam_init,
        'attn_norm_w': attn_norm_w[0], 'w_in': w_in[0].astype(BF16),
        's5': {tc: _s5_params(*s5_args, tc) for tc in {S5_CHUNK, t_dec}}, 'd_skip': ssm_d[0],
        'glu_w': ssm_glu_w[0].astype(BF16), 'glu_b': ssm_glu_b[0], 'ssm_norm_w': ssm_norm_w[0],
        'subln_w': subln_w[0], 'w_out': w_out[0].astype(BF16), 'ffn_norm_w': ffn_norm_w[0],
        'router_w': jnp.pad(router_w[0], ((0, 0), (0, LANES - ne))).astype(BF16),
        'router_b': jnp.pad(router_b[0], (0, LANES - ne), constant_values=NEG_BIG).reshape(1, LANES),
        'w_up': _wperm(w_up[0]),
        'b_up': jnp.concatenate([b_up[0, :, 0::2], b_up[0, :, 1::2]], axis=-1),
        'w_down': w_down[0].astype(BF16), 'b_down': b_down[0], 'final_norm_w': final_norm_w,
    }
    nb = x_prompt.shape[0]
    zeros = jnp.zeros((nb, g, SSM_STATE), F32)
    yp, kp, vp, srp, sip = _group(x_prompt, zeros, zeros, None, S5_CHUNK, wts)
    ys, ks, vs, srs, sis = _group(x_sample, state_ssm_re[0], state_ssm_im[0],
                                  (cache_k[0], cache_v[0], page_table), t_dec, wts)
    return (yp, ys, kp, vp, srp, sip, ks, vs, srs, sis)
```

```python
import functools
import math

import jax
import jax.numpy as jnp
from jax import lax
from jax.experimental import pallas as pl
from jax.experimental.pallas import tpu as pltpu

F32 = jnp.float32
BF16 = jnp.bfloat16

SSM_GROUP = 16
SSM_STATE = 64
ATT_HEADS = 4
TOP_K = 4
SWIGLU_LIMIT = 7.0
SWIGLU_ALPHA = 1.702
RMS_EPS = 1e-6
NEG_BIG = -1e30

LANES = 128
SUBLANES = 8
VMEM_LIMIT = 56 * 1024 * 1024

INPROJ_ROWS = 512
ATTN_Q = 256
ATTN_K = 512
S5_CHUNK = 16
S5_ROWS = 512
MIX_TILE = 512
TOKEN_TILE = 256
FFN_ROWS = 512
UNIT = SUBLANES
DEC_PAGES = 16


def _cparams(sem):
    return pltpu.CompilerParams(dimension_semantics=sem, vmem_limit_bytes=VMEM_LIMIT)


def _rms(x, w):
    ms = jnp.mean(x * x, axis=-1, keepdims=True)
    return x * lax.rsqrt(ms + RMS_EPS) * w


def _dot(a, b):
    return jnp.dot(a, b, preferred_element_type=F32)


def _dot_nt(a, b):
    return lax.dot_general(a, b, (((1,), (1,)), ((), ())), preferred_element_type=F32)


def _pack_pairs(x):
    half = x.shape[1] // 2
    lo = lax.bitcast_convert_type(x[:, :half], jnp.uint32)
    hi = lax.bitcast_convert_type(x[:, half:], jnp.uint32)
    return lax.shift_right_logical(lo, jnp.uint32(16)) | (hi & jnp.uint32(0xFFFF0000))


def _unpack_pairs(w):
    lo = lax.bitcast_convert_type(lax.shift_left(w, jnp.uint32(16)), F32)
    hi = lax.bitcast_convert_type(w & jnp.uint32(0xFFFF0000), F32)
    return jnp.concatenate([lo, hi], axis=1)


def _lane_mask(shape, lo, width):
    lane = lax.broadcasted_iota(jnp.int32, shape, 1)
    return (lane >= lo) & (lane < lo + width)


def _inproj_body(x_ref, nw_ref, w_ref, *refs, sw, aw, scale, pack):
    xn = _rms(x_ref[...], nw_ref[...])
    proj = _dot(xn.astype(BF16), w_ref[...])
    q = proj[:, sw:sw + aw]
    k = proj[:, sw + aw:sw + 2 * aw]
    v = proj[:, sw + 2 * aw:]
    if not pack:
        u_ref, k_ref, v_ref, qb_ref, kb_ref, vb_ref = refs
        k_ref[...] = k
        v_ref[...] = v
        vb_ref[...] = v.astype(BF16)
    else:
        u_ref, kt_ref, v4_ref, qb_ref, kb_ref, vt_ref, u2_ref, u_s = refs
        for vt in range(sw // LANES):
            u_s[vt] = proj[:, vt * LANES:(vt + 1) * LANES]
        vtr = v.T
        vt_ref[0] = vtr.astype(BF16)
        kt_ref[0] = k.T
        vd = aw // ATT_HEADS
        for h in range(ATT_HEADS):
            v4_ref[pl.ds(h, v.shape[0], stride=ATT_HEADS), :] = v[:, h * vd:(h + 1) * vd]
    u_ref[...] = proj[:, :sw]
    qb_ref[...] = (q * scale).astype(BF16)
    kb_ref[...] = k.astype(BF16)
    if pack:
        nk = u_s.shape[1] // S5_CHUNK
        per = LANES // SSM_GROUP
        for vt in range(sw // LANES):
            for half in range(S5_CHUNK // per):
                accs = [None] * per
                for s8 in range(per):
                    xv = u_s[vt, pl.ds(half * per + s8, nk, stride=S5_CHUNK), :]
                    mask = _lane_mask(xv.shape, s8 * SSM_GROUP, SSM_GROUP)
                    for gp in range(per):
                        shift = (SSM_GROUP * (s8 - gp)) % LANES
                        r = pltpu.roll(xv, shift, axis=1) if shift else xv
                        accs[gp] = r if s8 == 0 else jnp.where(mask, r, accs[gp])
                for gp in range(per):
                    u2_ref[vt * per + gp, :, half * LANES:(half + 1) * LANES] = accs[gp].astype(BF16)


def _inproj(x, norm_w, w_in_b, sw, aw, scale, pack, l=None):
    t, d = x.shape
    tm = min(INPROJ_ROWS, t)
    row = lambda i: (i, 0)
    fix = lambda i: (0, 0)
    f32o = lambda w: jax.ShapeDtypeStruct((t, w), F32)
    b16o = lambda w: jax.ShapeDtypeStruct((t, w), BF16)
    if pack:
        per = l // tm
        vd = aw // ATT_HEADS
        kv_specs = [pl.BlockSpec((1, aw, tm), lambda i: (i // per, 0, i % per)),
                    pl.BlockSpec((tm * ATT_HEADS, vd), row)]
        kv_shape = [jax.ShapeDtypeStruct((t // l, aw, l), F32), jax.ShapeDtypeStruct((t * ATT_HEADS, vd), F32)]
    else:
        kv_specs = [pl.BlockSpec((tm, aw), row)] * 2
        kv_shape = [f32o(aw), f32o(aw)]
    out_specs = [pl.BlockSpec((tm, sw), row)] + kv_specs + [pl.BlockSpec((tm, aw), row)] * 2
    out_shape = [f32o(sw)] + kv_shape + [b16o(aw), b16o(aw)]
    scratch = []
    if pack:
        g = sw // SSM_GROUP
        kd = S5_CHUNK * SSM_GROUP
        out_specs += [pl.BlockSpec((1, aw, tm), lambda i: (i, 0, 0)),
                      pl.BlockSpec((g, tm // S5_CHUNK, kd), lambda i: (0, i, 0))]
        out_shape += [jax.ShapeDtypeStruct((t // tm, aw, tm), BF16),
                      jax.ShapeDtypeStruct((g, t // S5_CHUNK, kd), BF16)]
        scratch = [pltpu.VMEM((sw // LANES, tm, LANES), F32)]
    else:
        out_specs += [pl.BlockSpec((tm, aw), row)]
        out_shape += [b16o(aw)]
    return pl.pallas_call(
        functools.partial(_inproj_body, sw=sw, aw=aw, scale=scale, pack=pack),
        grid=(t // tm,),
        in_specs=[pl.BlockSpec((tm, d), row), pl.BlockSpec((1, d), fix), pl.BlockSpec(w_in_b.shape, fix)],
        out_specs=out_specs, out_shape=out_shape, scratch_shapes=scratch,
        compiler_params=_cparams(("parallel",)),
    )(x, norm_w.reshape(1, d), w_in_b)


def _s5_params(lam_re, lam_im, log_dt, b_re, b_im, c_re, c_im, tc):
    hp = lax.Precision.HIGHEST
    g, p = lam_re.shape
    c = b_re.shape[-1]
    dt = jnp.exp(log_dt)[:, None]
    mag = jnp.exp(lam_re * dt)
    a_re = mag * jnp.cos(lam_im * dt)
    a_im = mag * jnp.sin(lam_im * dt)
    den = lam_re * lam_re + lam_im * lam_im
    nr = a_re - 1.0
    cf_re = (nr * lam_re + a_im * lam_im) / den
    cf_im = (a_im * lam_re - nr * lam_im) / den
    bb_re = cf_re[..., None] * b_re - cf_im[..., None] * b_im
    bb_im = cf_re[..., None] * b_im + cf_im[..., None] * b_re
    pr, pi = [jnp.ones_like(a_re)], [jnp.zeros_like(a_re)]
    for _ in range(tc):
        pr, pi = pr + [pr[-1] * a_re - pi[-1] * a_im], pi + [pr[-1] * a_im + pi[-1] * a_re]
    pr = jnp.stack(pr)
    pi = jnp.stack(pi)
    ab_re = pr[:tc, :, :, None] * bb_re - pi[:tc, :, :, None] * bb_im
    ab_im = pr[:tc, :, :, None] * bb_im + pi[:tc, :, :, None] * bb_re
    kern = (jnp.einsum('gdp,tgpc->tgdc', c_re, ab_re, precision=hp)
            - jnp.einsum('gdp,tgpc->tgdc', c_im, ab_im, precision=hp))
    lag = jnp.arange(tc)[None, :] - jnp.arange(tc)[:, None]
    kt = kern[jnp.clip(lag, 0, tc - 1)]
    kt = jnp.where((lag >= 0)[:, :, None, None, None], kt, 0.0)
    m = kt.transpose(2, 0, 4, 1, 3).reshape(g, tc * c, tc * c)
    wr = ab_re[::-1].transpose(1, 0, 3, 2).reshape(g, tc * c, p)
    wi = ab_im[::-1].transpose(1, 0, 3, 2).reshape(g, tc * c, p)
    w = jnp.concatenate([wr, wi], axis=-1)
    wsw = jnp.concatenate([wi, wr], axis=-1)
    e_re = c_re[None] * pr[1:, :, None, :] - c_im[None] * pi[1:, :, None, :]
    e_im = c_re[None] * pi[1:, :, None, :] + c_im[None] * pr[1:, :, None, :]
    v = jnp.concatenate([e_re.transpose(1, 3, 0, 2).reshape(g, p, tc * c),
                         -e_im.transpose(1, 3, 0, 2).reshape(g, p, tc * c)], axis=1)
    ar = jnp.concatenate([pr[tc], pr[tc]], axis=-1)[:, None, :]
    ai = jnp.concatenate([-pi[tc], pi[tc]], axis=-1)[:, None, :]
    return m.astype(BF16), w.astype(BF16), wsw.astype(BF16), v.astype(BF16), ar, ai


def _s5_body(u_ref, m_ref, w_ref, wsw_ref, v_ref, ar_ref, ai_ref, h0_ref, h0s_ref,
             y_ref, hout_ref, h_s, hs_s, s_s, ssw_s, hin_s, *, cb, nb, seq_major):
    @pl.when(pl.program_id(1) == 0)
    def _():
        h_s[...] = h0_ref[0]
        hs_s[...] = h0s_ref[0]

    kd = u_ref.shape[-1]
    ub = u_ref[0].reshape(cb * nb, kd).astype(BF16)
    yi = _dot(ub, m_ref[0])
    s_s[...] = _dot(ub, w_ref[0])
    ssw_s[...] = _dot(ub, wsw_ref[0])
    ar = ar_ref[0]
    ai = ai_ref[0]
    h = h_s[...]
    hs = hs_s[...]
    for c in range(cb):
        idx = pl.ds(c, nb, stride=cb) if (seq_major and cb > 1) else pl.ds(c * nb, nb)
        hin_s[idx, :] = h
        h, hs = ar * h + ai * hs + s_s[idx, :], ar * hs - ai * h + ssw_s[idx, :]
    h_s[...] = h
    hs_s[...] = hs
    hout_ref[0] = h
    yo = _dot(hin_s[...].astype(BF16), v_ref[0])
    y_ref[0] = (yi + yo).reshape(y_ref.shape[1:])


def _s5(u_in, h0_re, h0_im, prm, n, l, tc, packed):
    m, w, wsw, v, ar, ai = prm
    g = m.shape[0]
    c = SSM_GROUP
    p2 = 2 * SSM_STATE
    nc = l // tc
    kd = tc * c
    cb = max(1, min(nc, S5_ROWS // n))
    assert nc % cb == 0 and n % SUBLANES == 0
    if packed:
        u2 = u_in.reshape(g, n, nc, kd)
        blk_shape, blk = (1, n, cb, kd), (lambda gi, j: (gi, 0, j, 0))
    else:
        u2 = u_in.reshape(n, nc, tc, g, c).transpose(3, 1, 0, 2, 4).reshape(g, nc, n, kd).astype(BF16)
        blk_shape, blk = (1, cb, n, kd), (lambda gi, j: (gi, j, 0, 0))
    h0 = jnp.concatenate([h0_re, h0_im], axis=-1).transpose(1, 0, 2)
    h0s = jnp.concatenate([h0_im, h0_re], axis=-1).transpose(1, 0, 2)
    par = lambda gi, j: (gi, 0, 0)
    y2, hout = pl.pallas_call(
        functools.partial(_s5_body, cb=cb, nb=n, seq_major=packed),
        grid=(g, nc // cb),
        in_specs=[pl.BlockSpec(blk_shape, blk),
                  pl.BlockSpec((1, kd, kd), par), pl.BlockSpec((1, kd, p2), par), pl.BlockSpec((1, kd, p2), par),
                  pl.BlockSpec((1, p2, kd), par),
                  pl.BlockSpec((1, 1, p2), par), pl.BlockSpec((1, 1, p2), par),
                  pl.BlockSpec((1, n, p2), par), pl.BlockSpec((1, n, p2), par)],
        out_specs=[pl.BlockSpec(blk_shape, blk), pl.BlockSpec((1, n, p2), par)],
        out_shape=[jax.ShapeDtypeStruct(u2.shape, F32), jax.ShapeDtypeStruct((g, n, p2), F32)],
        scratch_shapes=[pltpu.VMEM((n, p2), F32), pltpu.VMEM((n, p2), F32), pltpu.VMEM((cb * n, p2), F32),
                        pltpu.VMEM((cb * n, p2), F32), pltpu.VMEM((cb * n, p2), F32)],
        compiler_params=_cparams(("parallel", "arbitrary")),
    )(u2, m, w, wsw, v, ar, ai, h0, h0s)
    if packed:
        y = y2.reshape(g, n * nc, kd)
    else:
        y = y2.reshape(g, nc, n, tc, c).transpose(2, 1, 3, 0, 4).reshape(n * l, g * c)
    hout = hout.transpose(1, 0, 2)
    return y, hout[..., :SSM_STATE], hout[..., SSM_STATE:]


def _attn_body(lam_ref, q_ref, k_ref, vt_ref, swb_ref, o_ref, q2_s, m_s, acc_s, *, tq, tk, vd, post):
    qi = pl.program_id(2)
    hd = vd // 2
    q = q_ref[0]
    lane = lax.broadcasted_iota(jnp.int32, q.shape, 1)
    zero = jnp.zeros_like(q)
    q2_s[0:tq, :] = jnp.where(lane < hd, q, zero)
    q2_s[tq:, :] = jnp.where(lane >= hd, q, zero)
    m_s[...] = jnp.full(m_s.shape, NEG_BIG, F32)
    acc_s[...] = jnp.zeros(acc_s.shape, F32)
    ones = jnp.ones((vd, tk), BF16)

    def scores(j):
        start = pl.multiple_of(j * tk, tk)
        return _dot_nt(k_ref[0, pl.ds(start, tk), :], q2_s[...])

    def step(j, st2, off):
        vaug = jnp.concatenate([vt_ref[j], ones], axis=0)
        for c in range(2):
            cs = slice(c * tq, (c + 1) * tq)
            st = st2[:, cs]
            if off is not None:
                key = lax.broadcasted_iota(jnp.int32, st.shape, 0)
                row = lax.broadcasted_iota(jnp.int32, st.shape, 1)
                st = jnp.where(key <= row + off, st, NEG_BIG)
            m_old = m_s[:, cs]
            m_new = jnp.maximum(m_old, jnp.max(st, axis=0, keepdims=True))
            alpha = jnp.exp(m_old - m_new)
            p = jnp.exp(st - m_new).astype(BF16)
            acc_s[:, cs] = alpha * acc_s[:, cs] + _dot(vaug, p)
            m_s[:, cs] = m_new

    def full_step(j, st2):
        nxt = scores(j + 1)
        step(j, st2, None)
        return nxt

    nfull = (qi * tq) // tk
    st2 = lax.fori_loop(0, nfull, full_step, scores(0))
    step(nfull, st2, qi * tq - nfull * tk)
    acc = acc_s[...]
    ot = (acc[:vd, :tq] / acc[vd:vd + 1, :tq]) - lam_ref[0] * (acc[:vd, tq:] / acc[vd:vd + 1, tq:])
    ms = jnp.mean(ot * ot, axis=0, keepdims=True)
    ot = ot * lax.rsqrt(ms + RMS_EPS) * swb_ref[...] * post
    o_ref[0] = ot.T.astype(o_ref.dtype)


def _attn_prompt(qb, kb, vt, lam, subln_w, n, l, post):
    aw = qb.shape[-1]
    vd = aw // ATT_HEADS
    tk = vt.shape[-1]
    tq = min(ATTN_Q, l)
    assert l % tk == 0 and tk % tq == 0
    q3, k3 = qb.reshape(n, l, aw), kb.reshape(n, l, aw)
    swb = jnp.broadcast_to(subln_w.reshape(vd, 1), (vd, tq))
    return pl.pallas_call(
        functools.partial(_attn_body, tq=tq, tk=tk, vd=vd, post=post),
        grid=(n, ATT_HEADS, l // tq),
        in_specs=[pl.BlockSpec(memory_space=pltpu.SMEM),
                  pl.BlockSpec((1, tq, vd), lambda b, h, i: (b, i, h)),
                  pl.BlockSpec((1, l, vd), lambda b, h, i: (b, 0, h)),
                  pl.BlockSpec((l // tk, vd, tk), lambda b, h, i: (b, h, 0)),
                  pl.BlockSpec((vd, tq), lambda b, h, i: (0, 0))],
        out_specs=pl.BlockSpec((1, tq, vd), lambda b, h, i: (b, i, h)),
        out_shape=jax.ShapeDtypeStruct((n, l, aw), BF16),
        scratch_shapes=[pltpu.VMEM((2 * tq, vd), BF16), pltpu.VMEM((1, 2 * tq), F32),
                        pltpu.VMEM((2 * vd, 2 * tq), F32)],
        compiler_params=_cparams(("parallel", "parallel", "arbitrary")),
    )(lam.reshape(1), q3, k3, vt, swb).reshape(n * l, aw)


def _attn_dec_body(pt_ref, lam_ref, qr_ref, kn_ref, vn_ref, sw_ref, *rest, pg, t_new, vd, post):
    k_refs, v_refs = rest[:pg], rest[pg:2 * pg]
    o_ref, m_s, l_s, acc_s = rest[2 * pg:]
    j = pl.program_id(1)

    @pl.when(j == 0)
    def _():
        m_s[...] = jnp.full(m_s.shape, NEG_BIG, F32)
        l_s[...] = jnp.zeros(l_s.shape, F32)
        acc_s[...] = jnp.zeros(acc_s.shape, F32)

    qr = qr_ref[0]
    rows = 2 * t_new
    page = k_refs[0].shape[2]

    def update(s, vhs):
        m_old = m_s[...]
        m_new = jnp.maximum(m_old, jnp.max(s, axis=-1, keepdims=True))
        alpha = jnp.exp(m_old - m_new)
        p = jnp.exp(s - m_new).astype(BF16)
        l_s[...] = alpha * l_s[...] + jnp.sum(p.astype(F32), axis=-1, keepdims=True)
        for h in range(ATT_HEADS):
            hs = slice(h * rows, (h + 1) * rows)
            pv = _dot(p[hs, :page], vhs[h][0])
            for i in range(1, len(vhs[h])):
                pv = pv + _dot(p[hs, i * page:(i + 1) * page], vhs[h][i])
            acc_s[hs, :] = alpha[hs] * acc_s[hs, :] + pv
        m_s[...] = m_new

    s = jnp.concatenate([_dot(qr, kr[0].astype(BF16)) for kr in k_refs], axis=1)
    update(s, [[vr[0, pl.ds(h, page, stride=ATT_HEADS), :].astype(BF16) for vr in v_refs]
               for h in range(ATT_HEADS)])

    @pl.when(j == pl.num_programs(1) - 1)
    def _():
        sn = _dot_nt(qr, kn_ref[0])
        r = lax.broadcasted_iota(jnp.int32, sn.shape, 0) % t_new
        c = lax.broadcasted_iota(jnp.int32, sn.shape, 1)
        vn = vn_ref[0]
        update(jnp.where(c <= r, sn, NEG_BIG), [[vn[:, h * vd:(h + 1) * vd]] for h in range(ATT_HEADS)])
        acc = acc_s[...]
        l = l_s[...]
        for h in range(ATT_HEADS):
            blk = acc[h * rows:(h + 1) * rows] / l[h * rows:(h + 1) * rows]
            o = blk[:t_new] - lam_ref[0] * blk[t_new:]
            o_ref[0, :, h * vd:(h + 1) * vd] = _rms(o, sw_ref[...]) * post


def _attn_sample(qb, kb, vb, cache_k, cache_v, page_table, lam, subln_w, nseq, t_new, post):
    aw = qb.shape[-1]
    vd = aw // ATT_HEADS
    hd = vd // 2
    n_pool, page = cache_k.shape[0], cache_k.shape[1]
    n_pages = page_table.shape[1]
    pg = math.gcd(DEC_PAGES, n_pages)
    ck = cache_k.reshape(n_pool, page, aw).transpose(0, 2, 1)
    cv = cache_v.reshape(n_pool, page * ATT_HEADS, vd)
    q5 = qb.reshape(nseq, t_new, ATT_HEADS, 2, hd)
    qr = jnp.einsum('bqhcd,hx,cy->bhcqxyd', q5, jnp.eye(ATT_HEADS, dtype=BF16), jnp.eye(2, dtype=BF16))
    nr = ATT_HEADS * 2 * t_new
    qr = qr.reshape(nseq, nr, aw)
    pad = ((0, 0), (0, page - t_new), (0, 0))
    kn = jnp.pad(kb.reshape(nseq, t_new, aw), pad)
    vn = jnp.pad(vb.reshape(nseq, t_new, aw), pad)
    seq = lambda b, j, pt: (b, 0, 0)
    pick = lambda b, j, pt, i: (pt[b, j * pg + i], 0, 0)
    k_specs = [pl.BlockSpec((1, aw, page), functools.partial(pick, i=i)) for i in range(pg)]
    v_specs = [pl.BlockSpec((1, page * ATT_HEADS, vd), functools.partial(pick, i=i)) for i in range(pg)]
    return pl.pallas_call(
        functools.partial(_attn_dec_body, pg=pg, t_new=t_new, vd=vd, post=post),
        grid_spec=pltpu.PrefetchScalarGridSpec(
            num_scalar_prefetch=1,
            grid=(nseq, n_pages // pg),
            in_specs=[pl.BlockSpec(memory_space=pltpu.SMEM),
                      pl.BlockSpec((1, nr, aw), seq), pl.BlockSpec((1, page, aw), seq),
                      pl.BlockSpec((1, page, aw), seq), pl.BlockSpec((1, vd), lambda b, j, pt: (0, 0))]
            + k_specs + v_specs,
            out_specs=pl.BlockSpec((1, t_new, aw), seq),
            scratch_shapes=[pltpu.VMEM((nr, 1), F32), pltpu.VMEM((nr, 1), F32), pltpu.VMEM((nr, vd), F32)]),
        out_shape=jax.ShapeDtypeStruct((nseq, t_new, aw), F32),
        compiler_params=_cparams(("parallel", "arbitrary")),
    )(page_table, lam.reshape(1), qr, kn, vn, subln_w.reshape(1, vd), *([ck] * pg), *([cv] * pg)
      ).reshape(nseq * t_new, aw)


def _mix_body(ys_ref, u_ref, dsk_ref, o_ref, x_ref, gw_ref, gb_ref, snw_ref, wo_ref, fnw_ref, rw_ref, rb_ref,
              h_ref, xn_ref, info_ref, infot_ref, *scratch, sw, packed):
    if packed:
        ys_s, = scratch
        nk = ys_s.shape[1] // S5_CHUNK
        per = LANES // SSM_GROUP
        for vt in range(sw // LANES):
            for half in range(S5_CHUNK // per):
                srcs = [ys_ref[vt * per + gp, :, half * LANES:(half + 1) * LANES] for gp in range(per)]
                masks = [_lane_mask(srcs[0].shape, gp * SSM_GROUP, SSM_GROUP) for gp in range(per)]
                for s8 in range(per):
                    val = None
                    for gp in range(per):
                        shift = (SSM_GROUP * (gp - s8)) % LANES
                        r = pltpu.roll(srcs[gp], shift, axis=1) if shift else srcs[gp]
                        val = r if gp == 0 else jnp.where(masks[gp], r, val)
                    ys_s[vt, pl.ds(half * per + s8, nk, stride=S5_CHUNK), :] = val
        ys = jnp.concatenate([ys_s[vt] for vt in range(sw // LANES)], axis=1)
    else:
        ys = ys_ref[...]
    y = jax.nn.gelu(ys + dsk_ref[...] * u_ref[...])
    z = _dot(y.astype(BF16), gw_ref[...]) + gb_ref[...]
    y = y * (1.0 / (1.0 + jnp.exp(-z)))
    yn = _rms(y, snw_ref[...])
    mixed = _dot(yn.astype(BF16), wo_ref[:sw, :]) + _dot(o_ref[...].astype(BF16), wo_ref[sw:, :])
    h = x_ref[...] + mixed
    h_ref[...] = h
    xnb = _rms(h, fnw_ref[...]).astype(BF16)
    xn_ref[...] = xnb
    logits = _dot(xnb, rw_ref[...]) + rb_ref[...]
    lane = lax.broadcasted_iota(jnp.int32, logits.shape, 1).astype(F32)
    work = logits
    vals, idxs = [], []
    for _ in range(TOP_K):
        mx = jnp.max(work, axis=-1, keepdims=True)
        ix = jnp.min(jnp.where(work == mx, lane, float(LANES)), axis=-1, keepdims=True)
        vals.append(mx)
        idxs.append(ix)
        work = jnp.where(lane == ix, -3e38, work)
    es = [jnp.exp(v - vals[0]) for v in vals]
    den = es[0] + es[1] + es[2] + es[3]
    info = jnp.zeros(logits.shape, F32)
    for k in range(TOP_K):
        info = info + jnp.where(lane == float(k), idxs[k], 0.0) + jnp.where(lane == float(TOP_K + k), es[k] / den, 0.0)
    info_ref[...] = info[:, :2 * TOP_K]
    tt = infot_ref.shape[2]
    for s in range(infot_ref.shape[0]):
        infot_ref[s] = info[s * tt:(s + 1) * tt].T[:2 * TOP_K, :]


def _mix(ys, u, d_skip, o, x, glu_w_b, glu_b, ssm_norm_w, w_out_b, ffn_norm_w, router_w_b, router_b_p, packed):
    t, d = x.shape
    sw = u.shape[1]
    aw = o.shape[1]
    tm = min(MIX_TILE, t)
    tt = min(TOKEN_TILE, t)
    nt = t // tm
    row = lambda i: (i, 0)
    fix = lambda i: (0, 0)
    if packed:
        ys_spec = pl.BlockSpec((ys.shape[0], tm // S5_CHUNK, ys.shape[2]), lambda i: (0, i, 0))
        scratch = [pltpu.VMEM((sw // LANES, tm, LANES), F32)]
    else:
        ys_spec = pl.BlockSpec((tm, sw), row)
        scratch = []
    return pl.pallas_call(
        functools.partial(_mix_body, sw=sw, packed=packed),
        grid=(nt,),
        in_specs=[ys_spec, pl.BlockSpec((tm, sw), row), pl.BlockSpec((1, sw), fix),
                  pl.BlockSpec((tm, aw), row), pl.BlockSpec((tm, d), row),
                  pl.BlockSpec((sw, sw), fix), pl.BlockSpec((1, sw), fix), pl.BlockSpec((1, sw), fix),
                  pl.BlockSpec((d, d), fix), pl.BlockSpec((1, d), fix),
                  pl.BlockSpec((d, LANES), fix), pl.BlockSpec((1, LANES), fix)],
        out_specs=[pl.BlockSpec((tm, d), row), pl.BlockSpec((tm, d), row),
                   pl.BlockSpec((tm, 2 * TOP_K), row),
                   pl.BlockSpec((tm // tt, 2 * TOP_K, tt), lambda i: (i, 0, 0))],
        out_shape=[jax.ShapeDtypeStruct((t, d), F32), jax.ShapeDtypeStruct((t, d), BF16),
                   jax.ShapeDtypeStruct((t, 2 * TOP_K), F32), jax.ShapeDtypeStruct((t // tt, 2 * TOP_K, tt), F32)],
        scratch_shapes=scratch,
        compiler_params=_cparams(("parallel",)),
    )(ys, u, d_skip.reshape(1, sw), o, x, glu_w_b, glu_b.reshape(1, sw), ssm_norm_w.reshape(1, sw), w_out_b,
      ffn_norm_w.reshape(1, d), router_w_b, router_b_p)


def _wperm_body(w_ref, p_ref, o_ref):
    f = o_ref.shape[-1] // 2
    grp = 2 * LANES
    for k in range(w_ref.shape[-1] // grp):
        r = _dot(w_ref[0, :, k * grp:(k + 1) * grp].astype(BF16), p_ref[...]).astype(BF16)
        o_ref[0, :, k * LANES:(k + 1) * LANES] = r[:, :LANES]
        o_ref[0, :, f + k * LANES:f + (k + 1) * LANES] = r[:, LANES:]


def _wperm(w_up):
    ne, d, f2 = w_up.shape
    grp = 2 * LANES
    src = jnp.arange(grp)[:, None]
    dst = jnp.arange(grp)[None, :]
    perm = (dst == (src % 2) * LANES + src // 2).astype(BF16)
    return pl.pallas_call(
        _wperm_body,
        grid=(ne,),
        in_specs=[pl.BlockSpec((1, d, f2), lambda e: (e, 0, 0)), pl.BlockSpec((grp, grp), lambda e: (0, 0))],
        out_specs=pl.BlockSpec((1, d, f2), lambda e: (e, 0, 0)),
        out_shape=jax.ShapeDtypeStruct((ne, d, f2), BF16),
        compiler_params=_cparams(("parallel",)),
    )(w_up, perm)


def _seg_rows(tt, ne):
    worst = tt * TOP_K + ne * (UNIT - 1)
    return -(-worst // LANES) * LANES


def _moe_sort_body(x_ref, infot_ref, xs_ref, cp_ref, pos_ref, *, tt, ne, rt):
    infot = infot_ref[0]
    sub = lax.broadcasted_iota(jnp.int32, (ne, tt), 0).astype(F32)
    onehots = [(sub == infot[k:k + 1, :]).astype(F32) for k in range(TOP_K)]
    et = onehots[0] + onehots[1] + onehots[2] + onehots[3]
    cnt = jnp.sum(et, axis=1, keepdims=True)
    cp = jnp.floor((cnt + (UNIT - 1.0)) * (1.0 / UNIT))
    er = lax.broadcasted_iota(jnp.int32, (ne, ne), 0)
    ec = lax.broadcasted_iota(jnp.int32, (ne, ne), 1)
    cpb = jnp.broadcast_to(cp, (ne, LANES))
    lo = _dot((ec < er).astype(BF16), cpb.astype(BF16))[:, :1] * float(UNIT)
    tr = lax.broadcasted_iota(jnp.int32, (tt, tt), 0)
    tc = lax.broadcasted_iota(jnp.int32, (tt, tt), 1)
    rank = _dot(et.astype(BF16), (tr < tc).astype(BF16))
    base = lo + rank
    poss = [jnp.sum(oh * base, axis=0, keepdims=True) for oh in onehots]
    rows = lax.broadcasted_iota(jnp.int32, (rt, tt), 0).astype(F32)
    sel = (rows == poss[0]).astype(F32)
    for k in range(1, TOP_K):
        sel = sel + (rows == poss[k]).astype(F32)
    xs_ref[...] = _pack_pairs(_dot(sel.astype(BF16), x_ref[...]))
    cp_ref[0] = cpb.astype(jnp.int32)
    pt = jnp.concatenate(poss + [jnp.zeros((LANES - TOP_K, tt), F32)], axis=0)
    pos_ref[...] = pt.T[:, :2 * TOP_K]


def _moe_sort(xn, infot, ne):
    t, d = xn.shape
    nt, _, tt = infot.shape
    rt = _seg_rows(tt, ne)
    return pl.pallas_call(
        functools.partial(_moe_sort_body, tt=tt, ne=ne, rt=rt),
        grid=(nt,),
        in_specs=[pl.BlockSpec((tt, d), lambda i: (i, 0)), pl.BlockSpec((1, 2 * TOP_K, tt), lambda i: (i, 0, 0))],
        out_specs=[pl.BlockSpec((rt, d // 2), lambda i: (i, 0)), pl.BlockSpec((1, ne, LANES), lambda i: (i, 0, 0)),
                   pl.BlockSpec((tt, 2 * TOP_K), lambda i: (i, 0))],
        out_shape=[jax.ShapeDtypeStruct((nt * rt, d // 2), jnp.uint32),
                   jax.ShapeDtypeStruct((nt, ne, LANES), jnp.int32),
                   jax.ShapeDtypeStruct((t, 2 * TOP_K), F32)],
        compiler_params=_cparams(("parallel",)),
    )(xn, infot)


def _moe_tables(cp, rt, bu):
    nt, ne = cp.shape
    ru = rt // UNIT
    lo = jnp.cumsum(cp, axis=1) - cp
    used = jnp.sum(cp, axis=1)
    seg_len = jnp.concatenate([cp.T, (ru - used)[None, :]], axis=0)
    seg_src = jnp.concatenate([(jnp.arange(nt)[:, None] * ru + lo).T,
                               (jnp.arange(nt) * ru + used)[None, :]], axis=0)
    seg_off = jnp.cumsum(seg_len, axis=1) - seg_len
    tot = jnp.sum(seg_len, axis=1)
    eblk = -(-tot // bu)
    bend = jnp.cumsum(eblk)
    bstart = bend - eblk
    nblk = (nt * ru) // bu + ne + 1
    blk = jnp.arange(nblk, dtype=jnp.int32)
    e = jnp.sum((bend[None, :] <= blk[:, None]).astype(jnp.int32), axis=1)
    ec = jnp.minimum(e, ne)
    w = (blk - bstart[ec])[:, None] * bu + jnp.arange(bu, dtype=jnp.int32)[None, :]
    is_pad = (e > ne)[:, None] | (w >= tot[ec][:, None])
    so = seg_off[ec]
    sv = (seg_src - seg_off)[ec]
    inside = so[:, None, :] <= w[:, :, None]
    last = inside & ~jnp.concatenate([inside[:, :, 1:], jnp.zeros_like(inside[:, :, :1])], axis=2)
    unit = (jnp.sum(jnp.where(last, sv[:, None, :], 0), axis=2) + w).astype(jnp.int32)
    dump = nt * ru + jnp.arange(bu, dtype=jnp.int32)[None, :]
    tab = jnp.concatenate([jnp.where(is_pad, 0, unit), jnp.where(is_pad, dump, unit)], axis=1)
    pad_l = -(-2 * bu // LANES) * LANES - 2 * bu
    tab = jnp.pad(tab, ((0, 0), (0, pad_l))).reshape(nblk, 1, -1).astype(jnp.int32)
    nact = bend[ne].astype(jnp.int32).reshape(1)
    return tab, e.astype(jnp.int32), nact, nblk


def _moe_ffn_body(bexp_ref, nact_ref, tab_ref, tabn_ref, xs_hbm, wup_ref, bup_ref, wdn_ref, bdn_ref,
                  out_hbm, xbuf, obuf, sem_in, sem_out, *, bu, ne):
    b = pl.program_id(0)
    nact = nact_ref[0]
    slot = b % 2
    dw = xbuf.shape[-1]

    def start_in(tref, s):
        for j in range(bu):
            pltpu.make_async_copy(xs_hbm.at[tref[0, 0, j]], xbuf.at[s, j], sem_in.at[s]).start()

    def wait_in(s):
        for j in range(bu):
            pltpu.make_async_copy(xs_hbm.at[0], xbuf.at[s, j], sem_in.at[s]).wait()

    def start_out():
        for j in range(bu):
            pltpu.make_async_copy(obuf.at[j], out_hbm.at[tab_ref[0, 0, bu + j]], sem_out).start()

    def wait_out():
        for j in range(bu):
            pltpu.make_async_copy(obuf.at[j], out_hbm.at[0], sem_out).wait()

    @pl.when(b < nact)
    def _():
        @pl.when(b == 0)
        def _():
            start_in(tab_ref, 0)

        wait_in(slot)

        @pl.when(b + 1 < nact)
        def _():
            start_in(tabn_ref, 1 - slot)

        e = bexp_ref[b]

        @pl.when(e < ne)
        def _():
            x = _unpack_pairs(xbuf[slot].reshape(bu * UNIT, dw)).astype(BF16)
            h = _dot(x, wup_ref[0]) + bup_ref[0]
            f = h.shape[1] // 2
            glu = jnp.minimum(h[:, :f], SWIGLU_LIMIT)
            lin = jnp.clip(h[:, f:], -SWIGLU_LIMIT, SWIGLU_LIMIT)
            act = glu * (1.0 / (1.0 + jnp.exp(-SWIGLU_ALPHA * glu))) * (lin + 1.0)
            out = _dot(act.astype(BF16), wdn_ref[0]) + bdn_ref[0]

            @pl.when(b > 0)
            def _():
                wait_out()

            obuf[...] = _pack_pairs(out.astype(BF16).astype(F32)).reshape(bu, UNIT, dw)

        @pl.when(e >= ne)
        def _():
            @pl.when(b > 0)
            def _():
                wait_out()

            obuf[...] = jnp.zeros(obuf.shape, jnp.uint32)

        start_out()

        @pl.when(b == nact - 1)
        def _():
            wait_out()
            nu = out_hbm.shape[0] - bu
            for j in range(bu):
                pltpu.make_async_copy(obuf.at[j], out_hbm.at[nu + j], sem_out).start()
            wait_out()


def _moe_ffn(xs, cp, wup_b, bup, wdn_b, bdn, rt):
    ne, d, f2 = wup_b.shape
    bu = FFN_ROWS // UNIT
    tab, bexp, nact, nblk = _moe_tables(cp, rt, bu)
    nu = xs.shape[0] // UNIT
    dw = xs.shape[1]
    xs3 = xs.reshape(nu, UNIT, dw)
    tl = tab.shape[-1]
    wsel = lambda b, be, na: (jnp.minimum(be[b], ne - 1), 0, 0)
    out = pl.pallas_call(
        functools.partial(_moe_ffn_body, bu=bu, ne=ne),
        grid_spec=pltpu.PrefetchScalarGridSpec(
            num_scalar_prefetch=2,
            grid=(nblk,),
            in_specs=[pl.BlockSpec((1, 1, tl), lambda b, be, na: (b, 0, 0), memory_space=pltpu.SMEM),
                      pl.BlockSpec((1, 1, tl), lambda b, be, na: (jnp.minimum(b + 1, nblk - 1), 0, 0),
                                   memory_space=pltpu.SMEM),
                      pl.BlockSpec(memory_space=pl.ANY),
                      pl.BlockSpec((1, d, f2), wsel), pl.BlockSpec((1, 1, f2), wsel),
                      pl.BlockSpec((1, f2 // 2, d), wsel), pl.BlockSpec((1, 1, d), wsel)],
            out_specs=pl.BlockSpec(memory_space=pl.ANY),
            scratch_shapes=[pltpu.VMEM((2, bu, UNIT, dw), jnp.uint32), pltpu.VMEM((bu, UNIT, dw), jnp.uint32),
                            pltpu.SemaphoreType.DMA((2,)), pltpu.SemaphoreType.DMA(())]),
        out_shape=jax.ShapeDtypeStruct((nu + bu, UNIT, dw), jnp.uint32),
        compiler_params=_cparams(("arbitrary",)),
    )(bexp, nact, tab, tab, xs3, wup_b, bup.reshape(ne, 1, f2), wdn_b, bdn.reshape(ne, 1, d))
    return out.reshape((nu + bu) * UNIT, dw)


def _moe_combine_body(o_ref, pos_ref, info_ref, h_ref, fw_ref, y_ref, *, rt):
    pos = pos_ref[...]
    info = info_ref[...]
    lane = lax.broadcasted_iota(jnp.int32, (pos.shape[0], rt), 1).astype(F32)
    selt = jnp.where(lane == pos[:, 0:1], info[:, TOP_K:TOP_K + 1], 0.0)
    for k in range(1, TOP_K):
        selt = selt + jnp.where(lane == pos[:, k:k + 1], info[:, TOP_K + k:TOP_K + k + 1], 0.0)
    y = _dot(selt.astype(BF16), _unpack_pairs(o_ref[...]).astype(BF16))
    y_ref[...] = _rms(h_ref[...] + y, fw_ref[...])


def _moe_combine(outs, pos, info, h, final_w, rt):
    t, d = h.shape
    tt = min(TOKEN_TILE, t)
    row = lambda i: (i, 0)
    return pl.pallas_call(
        functools.partial(_moe_combine_body, rt=rt),
        grid=(t // tt,),
        in_specs=[pl.BlockSpec((rt, d // 2), row), pl.BlockSpec((tt, 2 * TOP_K), row), pl.BlockSpec((tt, 2 * TOP_K), row),
                  pl.BlockSpec((tt, d), row), pl.BlockSpec((1, d), lambda i: (0, 0))],
        out_specs=pl.BlockSpec((tt, d), row),
        out_shape=jax.ShapeDtypeStruct((t, d), F32),
        compiler_params=_cparams(("parallel",)),
    )(outs, pos, info, h, final_w.reshape(1, d))


def _group(x3, h0_re, h0_im, past, tc, wts):
    n, l, d = x3.shape
    x = x3.reshape(n * l, d)
    sw, aw = wts['sw'], wts['aw']
    hd = aw // (2 * ATT_HEADS)
    prompt = past is None
    proj = _inproj(x, wts['attn_norm_w'], wts['w_in'], sw, aw, hd ** -0.5, pack=prompt, l=l)
    u, k, v, qb, kb = proj[:5]
    heads = (n, l, ATT_HEADS)
    if prompt:
        vt, u2 = proj[5:]
        k_out = k.reshape(n, ATT_HEADS, 2, hd, l).transpose(0, 4, 1, 2, 3)
        v_out = v.reshape(heads + (2 * hd,))
        ys, st_re, st_im = _s5(u2, h0_re, h0_im, wts['s5'][tc], n, l, tc, packed=True)
        o = _attn_prompt(qb, kb, vt, wts['lam'], wts['subln_w'], n, l, wts['post'])
    else:
        k_out = k.reshape(heads + (2, hd))
        v_out = v.reshape(heads + (2 * hd,))
        ys, st_re, st_im = _s5(u, h0_re, h0_im, wts['s5'][tc], n, l, tc, packed=False)
        o = _attn_sample(qb, kb, proj[5], past[0], past[1], past[2], wts['lam'], wts['subln_w'], n, l, wts['post'])
    h, xn, info, infot = _mix(ys, u, wts['d_skip'], o, x, wts['glu_w'], wts['glu_b'], wts['ssm_norm_w'],
                              wts['w_out'], wts['ffn_norm_w'], wts['router_w'], wts['router_b'], packed=prompt)
    ne = wts['w_up'].shape[0]
    rt = _seg_rows(infot.shape[2], ne)
    xs, cpl, pos = _moe_sort(xn, infot, ne)
    outs = _moe_ffn(xs, cpl[:, :, 0], wts['w_up'], wts['b_up'], wts['w_down'], wts['b_down'], rt)
    y = _moe_combine(outs, pos, info, h, wts['final_norm_w'], rt)
    return y.reshape(n, l, d), k_out[None], v_out[None], st_re[None], st_im[None]


def kernel(x_prompt, x_sample, cache_k, cache_v, state_ssm_re, state_ssm_im, page_table, attn_norm_w, w_in, ssm_lambda_re, ssm_lambda_im, ssm_log_dt, ssm_b_re, ssm_b_im, ssm_c_re, ssm_c_im, ssm_d, ssm_glu_w, ssm_glu_b, ssm_norm_w, diff_lambda_q1, diff_lambda_k1, diff_lambda_q2, diff_lambda_k2, subln_w, w_out, ffn_norm_w, router_w, router_b, w_up, b_up, w_down, b_down, final_norm_w):
    assert w_in.shape[0] == 1, "single-layer trunk"
    g = ssm_lambda_re.shape[1]
    sw = g * SSM_GROUP
    aw = (w_in.shape[-1] - sw) // 3
    ne = router_w.shape[-1]
    lam_init = 0.8 - 0.6 * math.exp(-0.3 * 0)
    lam = (jnp.exp(jnp.sum(diff_lambda_q1[0] * diff_lambda_k1[0]))
           - jnp.exp(jnp.sum(diff_lambda_q2[0] * diff_lambda_k2[0])) + lam_init).astype(F32)
    t_dec = x_sample.shape[1]
    s5_args = (ssm_lambda_re[0], ssm_lambda_im[0], ssm_log_dt[0], ssm_b_re[0], ssm_b_im[0],
               ssm_c_re[0], ssm_c_im[0])
    wts = {
        'sw': sw, 'aw': aw, 'lam': lam, 'post': 1.0 - lam_init,
        'attn_norm_w': attn_norm_w[0], 'w_in': w_in[0].astype(BF16),
        's5': {tc: _s5_params(*s5_args, tc) for tc in {S5_CHUNK, t_dec}}, 'd_skip': ssm_d[0],
        'glu_w': ssm_glu_w[0].astype(BF16), 'glu_b': ssm_glu_b[0], 'ssm_norm_w': ssm_norm_w[0],
        'subln_w': subln_w[0], 'w_out': w_out[0].astype(BF16), 'ffn_norm_w': ffn_norm_w[0],
        'router_w': jnp.pad(router_w[0], ((0, 0), (0, LANES - ne))).astype(BF16),
        'router_b': jnp.pad(router_b[0], (0, LANES - ne), constant_values=NEG_BIG).reshape(1, LANES),
        'w_up': _wperm(w_up[0]),
        'b_up': jnp.concatenate([b_up[0, :, 0::2], b_up[0, :, 1::2]], axis=-1),
        'w_down': w_down[0].astype(BF16), 'b_down': b_down[0], 'final_norm_w': final_norm_w,
    }
    nb = x_prompt.shape[0]
    zeros = jnp.zeros((nb, g, SSM_STATE), F32)
    yp, kp, vp, srp, sip = _group(x_prompt, zeros, zeros, None, S5_CHUNK, wts)
    ys, ks, vs, srs, sis = _group(x_sample, state_ssm_re[0], state_ssm_im[0],
                                  (cache_k[0], cache_v[0], page_table), t_dec, wts)
    return (yp, ys, kp, vp, srp, sip, ks, vs, srs, sis)
```
